```python
import math
import jax, jax.numpy as jnp
from jax import lax
import numpy as np

D_MODEL = 1024
BATCH = 8
SEQ = 2048
DEPTH = 1
DEC_BATCH = 4
DEC_SEQ = 8192
PAST_LEN = 128

GLA_HEADS = 4
GLA_DK = 64
GLA_DV = 128
GLA_QK_WIDTH = GLA_HEADS * GLA_DK
GLA_WIDTH = GLA_HEADS * GLA_DV
GLA_GATE_RANK = 16
GLA_GATE_TAU = 16.0
GLA_CHUNK = 64
RWKV_HEAD = 64
RWKV_WIDTH = D_MODEL - GLA_WIDTH
RWKV_HEADS = RWKV_WIDTH // RWKV_HEAD
RWKV_DECAY_RANK = 64
RWKV_AAA_RANK = 64
RWKV_GATE_RANK = 128
RWKV_GN_EPS = 64e-5
MIX_WIDTH = GLA_WIDTH + RWKV_WIDTH
GLA_SPLITS = (GLA_QK_WIDTH, GLA_QK_WIDTH, GLA_WIDTH, GLA_WIDTH, GLA_GATE_RANK, GLA_GATE_RANK)
RWKV_SPLITS = (RWKV_WIDTH, RWKV_WIDTH, RWKV_WIDTH, RWKV_DECAY_RANK, RWKV_DECAY_RANK, RWKV_AAA_RANK, RWKV_GATE_RANK)
GLA_COLS = sum(GLA_SPLITS)
RWKV_COLS = sum(RWKV_SPLITS)
IN_COLS = GLA_COLS + RWKV_COLS
D_FF = 2816
MEM_TOKENS = 256
MEM_HEADS = 4
MEM_HEAD_DIM = D_MODEL // MEM_HEADS
LN_EPS = 1e-5
DEEPNORM_ALPHA = (2.0 * DEPTH) ** 0.25
DEEPNORM_BETA = (8.0 * DEPTH) ** -0.25

kernel_name = 'hybrid_gla_rwkv7_macaron_encoder'


def _split(t, sizes):
    offs = np.cumsum(np.array(sizes))[:-1].tolist()
    return jnp.split(t, offs, axis=-1)


def _layer_norm(x, g, b, eps=LN_EPS):
    xf = x.astype(jnp.float32)
    mu = jnp.mean(xf, axis=-1, keepdims=True)
    var = jnp.mean(jnp.square(xf - mu), axis=-1, keepdims=True)
    return ((xf - mu) * lax.rsqrt(var + eps) * g + b).astype(x.dtype)


def _swiglu(x, w_in, w_out):
    gate, up = jnp.split(x @ w_in, 2, axis=-1)
    return (jax.nn.silu(gate) * up) @ w_out


def _gla_chunked(q, k, v, log_a):
    B, T, H, DK = q.shape
    DV = v.shape[-1]
    C = GLA_CHUNK
    n = T // C

    def blocks(t):
        return t.reshape(B, n, C, H, t.shape[-1]).transpose(0, 3, 1, 2, 4)

    q, k, v, log_a = blocks(q), blocks(k), blocks(v), blocks(log_a)
    b = jnp.cumsum(log_a, axis=3)
    b_last = b[:, :, :, -1:, :]
    b_ref = 0.5 * b_last
    scores = jnp.einsum('bhncd,bhnsd->bhncs', q * jnp.exp(b - b_ref), k * jnp.exp(b_ref - b))
    prefix_mask = jnp.tril(jnp.ones((C, C), dtype=bool))
    scores = jnp.where(prefix_mask, scores, 0.0)
    o_intra = jnp.einsum('bhncs,bhnse->bhnce', scores, v)
    kv = jnp.einsum('bhncd,bhnce->nbhde', k * jnp.exp(b_last - b), v)
    chunk_decay = jnp.exp(b_last[:, :, :, 0, :]).transpose(2, 0, 1, 3)

    def step(S, inp):
        dec, kv_n = inp
        return dec[..., None] * S + kv_n, S

    _, S_prev = lax.scan(step, jnp.zeros((B, H, DK, DV), jnp.float32), (chunk_decay, kv))
    o_inter = jnp.einsum('bhncd,nbhde->bhnce', q * jnp.exp(b), S_prev)
    o = o_intra + o_inter
    return o.transpose(0, 2, 3, 1, 4).reshape(B, T, H, DV)


def _rwkv7_scan(r, decay, k, v, kk, a, reverse):
    B, T, H, N = r.shape
    xs = tuple(t.transpose(1, 0, 2, 3) for t in (r, decay, k, v, kk, a))

    def step(S, inp):
        r_t, w_t, k_t, v_t, kk_t, a_t = inp
        sa = jnp.einsum('bhij,bhj->bhi', S, -kk_t)
        S = (S * w_t[:, :, None, :] + sa[..., None] * (kk_t * a_t)[:, :, None, :]
             + v_t[..., None] * k_t[:, :, None, :])
        y = jnp.einsum('bhij,bhj->bhi', S, r_t)
        return S, y

    _, y = lax.scan(step, jnp.zeros((B, H, N, N), jnp.float32), xs, reverse=reverse)
    return y.transpose(1, 0, 2, 3)


def _token_mix(h, w_in, gla_gate_up_fwd, gla_gate_b_fwd, gla_gate_up_bwd, gla_gate_b_bwd, gla_norm_g,
               rwkv_mu_prev, rwkv_mu_next, rwkv_w0_fwd, rwkv_w_up_fwd, rwkv_w0_bwd, rwkv_w_up_bwd,
               rwkv_a0, rwkv_a_up, rwkv_g_up, rwkv_k_k, rwkv_k_a, rwkv_r_k, rwkv_lnx_g, rwkv_lnx_b, w_out):
    B, T, _ = h.shape
    f32 = jnp.float32
    proj = h @ w_in
    gla_p, rw_p = proj[..., :GLA_COLS], proj[..., GLA_COLS:]

    def heads(t, d):
        return t.astype(f32).reshape(B, T, -1, d)

    flip = lambda t: jnp.flip(t, axis=1)

    q, k, v, g, gd_f, gd_b = _split(gla_p, GLA_SPLITS)
    q = heads(q, GLA_DK) * (GLA_DK ** -0.5)
    k = heads(k, GLA_DK)
    v = heads(v, GLA_DV)
    la_f = heads(jax.nn.log_sigmoid(gd_f @ gla_gate_up_fwd + gla_gate_b_fwd), GLA_DK) / GLA_GATE_TAU
    la_b = heads(jax.nn.log_sigmoid(gd_b @ gla_gate_up_bwd + gla_gate_b_bwd), GLA_DK) / GLA_GATE_TAU
    o = _gla_chunked(q, k, v, la_f) + flip(_gla_chunked(flip(q), flip(k), flip(v), flip(la_b)))
    o = o * lax.rsqrt(jnp.mean(jnp.square(o), axis=-1, keepdims=True) + LN_EPS)
    o = o.reshape(B, T, GLA_WIDTH) * gla_norm_g * jax.nn.silu(g.astype(f32))

    prev = jnp.pad(rw_p[:, :-1], ((0, 0), (1, 0), (0, 0)))
    nxt = jnp.pad(rw_p[:, 1:], ((0, 0), (0, 1), (0, 0)))
    rw = rw_p + rwkv_mu_prev * (prev - rw_p) + rwkv_mu_next * (nxt - rw_p)
    r, kr, vr, wd_f, wd_b, ad, gd = _split(rw, RWKV_SPLITS)

    def decay(wd, w0, up):
        w = -jax.nn.softplus(-(w0 + jnp.tanh(wd) @ up).astype(f32)) - 0.5
        return heads(jnp.exp(-jnp.exp(w)), RWKV_HEAD)

    dec_f = decay(wd_f, rwkv_w0_fwd, rwkv_w_up_fwd)
    dec_b = decay(wd_b, rwkv_w0_bwd, rwkv_w_up_bwd)
    a = jax.nn.sigmoid((rwkv_a0 + ad @ rwkv_a_up).astype(f32))
    gate = (jax.nn.sigmoid(gd) @ rwkv_g_up).astype(f32)
    kr = kr.astype(f32)
    kk = heads(kr * rwkv_k_k, RWKV_HEAD)
    kk = kk / jnp.maximum(jnp.sqrt(jnp.sum(jnp.square(kk), axis=-1, keepdims=True)), 1e-12)
    k2 = heads(kr * (1.0 + (a - 1.0) * rwkv_k_a), RWKV_HEAD)
    rH = heads(r, RWKV_HEAD)
    vH = heads(vr, RWKV_HEAD)
    aH = heads(a, RWKV_HEAD)
    y = (_rwkv7_scan(rH, dec_f, k2, vH, kk, aH, reverse=False)
         + _rwkv7_scan(rH, dec_b, k2, vH, kk, aH, reverse=True))
    mu = jnp.mean(y, axis=-1, keepdims=True)
    var = jnp.mean(jnp.square(y - mu), axis=-1, keepdims=True)
    y = ((y - mu) * lax.rsqrt(var + RWKV_GN_EPS)).reshape(B, T, RWKV_WIDTH) * rwkv_lnx_g + rwkv_lnx_b
    bonus = (jnp.sum(rH * k2 * rwkv_r_k, axis=-1, keepdims=True) * vH).reshape(B, T, RWKV_WIDTH)
    y = (y + bonus) * gate

    mixed = jnp.concatenate([o, y], axis=-1).astype(h.dtype)
    return mixed @ w_out


def _memory_attn(h, mem, mem_ln_g, mem_ln_b, w_q, w_kv, w_o):
    B, T, _ = h.shape
    M = mem.shape[1]
    m = _layer_norm(mem, mem_ln_g, mem_ln_b)
    q = (h @ w_q).reshape(B, T, MEM_HEADS, MEM_HEAD_DIM)
    k, v = jnp.split(m @ w_kv, 2, axis=-1)
    k = k.reshape(B, M, MEM_HEADS, MEM_HEAD_DIM)
    v = v.reshape(B, M, MEM_HEADS, MEM_HEAD_DIM)
    s = jnp.einsum('bthd,bmhd->bhtm', q, k).astype(jnp.float32) * (MEM_HEAD_DIM ** -0.5)
    p = jax.nn.softmax(s, axis=-1).astype(v.dtype)
    o = jnp.einsum('bhtm,bmhd->bthd', p, v).reshape(B, T, D_MODEL)
    return o @ w_o


def setup_inputs(seed: int = 0) -> dict:
    key = jax.random.key(seed)
    keys = iter(jax.random.split(key, 64))
    f32 = jnp.float32
    L = DEPTH

    def nrm(shape, scale):
        return scale * jax.random.normal(next(keys), shape, f32)

    def gain(n):
        return 1.0 + nrm((L, n), 0.02)

    def bias(n):
        return nrm((L, n), 0.02)

    d = D_MODEL
    return {
        'x_prompt': nrm((BATCH, SEQ, d), 1.0),
        'x_sample': nrm((DEC_BATCH, DEC_SEQ, d), 1.0),
        'mem_prompt': nrm((BATCH, MEM_TOKENS, d), 1.0),
        'mem_sample': nrm((DEC_BATCH, MEM_TOKENS, d), 1.0),
        'ffn1_w_in': nrm((L, d, 2 * D_FF), d ** -0.5),
        'ffn1_w_out': nrm((L, D_FF, d), DEEPNORM_BETA * D_FF ** -0.5),
        'ln_ffn1_g': gain(d),
        'ln_ffn1_b': bias(d),
        'mix_w_in': nrm((L, d, IN_COLS), d ** -0.5),
        'gla_gate_up_fwd': nrm((L, GLA_GATE_RANK, GLA_QK_WIDTH), GLA_GATE_RANK ** -0.5),
        'gla_gate_b_fwd': nrm((L, GLA_QK_WIDTH), 0.1),
        'gla_gate_up_bwd': nrm((L, GLA_GATE_RANK, GLA_QK_WIDTH), GLA_GATE_RANK ** -0.5),
        'gla_gate_b_bwd': nrm((L, GLA_QK_WIDTH), 0.1),
        'gla_norm_g': gain(GLA_WIDTH),
        'rwkv_mu_prev': jax.random.uniform(next(keys), (L, RWKV_COLS), f32, 0.1, 0.5),
        'rwkv_mu_next': jax.random.uniform(next(keys), (L, RWKV_COLS), f32, 0.1, 0.5),
        'rwkv_w0_fwd': -2.0 + nrm((L, RWKV_WIDTH), 0.5),
        'rwkv_w_up_fwd': nrm((L, RWKV_DECAY_RANK, RWKV_WIDTH), 0.1 * RWKV_DECAY_RANK ** -0.5),
        'rwkv_w0_bwd': -2.0 + nrm((L, RWKV_WIDTH), 0.5),
        'rwkv_w_up_bwd': nrm((L, RWKV_DECAY_RANK, RWKV_WIDTH), 0.1 * RWKV_DECAY_RANK ** -0.5),
        'rwkv_a0': nrm((L, RWKV_WIDTH), 0.1),
        'rwkv_a_up': nrm((L, RWKV_AAA_RANK, RWKV_WIDTH), 0.5 * RWKV_AAA_RANK ** -0.5),
        'rwkv_g_up': nrm((L, RWKV_GATE_RANK, RWKV_WIDTH), RWKV_GATE_RANK ** -0.5),
        'rwkv_k_k': 0.85 + nrm((L, RWKV_WIDTH), 0.05),
        'rwkv_k_a': 1.0 + nrm((L, RWKV_WIDTH), 0.05),
        'rwkv_r_k': nrm((L, RWKV_HEADS, RWKV_HEAD), 0.1),
        'rwkv_lnx_g': gain(RWKV_WIDTH),
        'rwkv_lnx_b': bias(RWKV_WIDTH),
        'mix_w_out': nrm((L, MIX_WIDTH, d), DEEPNORM_BETA * MIX_WIDTH ** -0.5),
        'ln_mix_g': gain(d),
        'ln_mix_b': bias(d),
        'mem_ln_g': gain(d),
        'mem_ln_b': bias(d),
        'ca_w_q': nrm((L, d, d), d ** -0.5),
        'ca_w_kv': nrm((L, d, 2 * d), d ** -0.5),
        'ca_w_o': nrm((L, d, d), DEEPNORM_BETA * d ** -0.5),
        'ln_ca_g': gain(d),
        'ln_ca_b': bias(d),
        'ffn2_w_in': nrm((L, d, 2 * D_FF), d ** -0.5),
        'ffn2_w_out': nrm((L, D_FF, d), DEEPNORM_BETA * D_FF ** -0.5),
        'ln_ffn2_g': gain(d),
        'ln_ffn2_b': bias(d),
    }


def reference(x_prompt, x_sample, mem_prompt, mem_sample,
              ffn1_w_in, ffn1_w_out, ln_ffn1_g, ln_ffn1_b,
              mix_w_in, gla_gate_up_fwd, gla_gate_b_fwd, gla_gate_up_bwd, gla_gate_b_bwd, gla_norm_g,
              rwkv_mu_prev, rwkv_mu_next, rwkv_w0_fwd, rwkv_w_up_fwd, rwkv_w0_bwd, rwkv_w_up_bwd,
              rwkv_a0, rwkv_a_up, rwkv_g_up, rwkv_k_k, rwkv_k_a, rwkv_r_k, rwkv_lnx_g, rwkv_lnx_b,
              mix_w_out, ln_mix_g, ln_mix_b,
              mem_ln_g, mem_ln_b, ca_w_q, ca_w_kv, ca_w_o, ln_ca_g, ln_ca_b,
              ffn2_w_in, ffn2_w_out, ln_ffn2_g, ln_ffn2_b):
    alpha = DEEPNORM_ALPHA

    def layer(x, mem, l):
        x = _layer_norm(alpha * x + 0.5 * _swiglu(x, ffn1_w_in[l], ffn1_w_out[l]), ln_ffn1_g[l], ln_ffn1_b[l])
        tm = _token_mix(x, mix_w_in[l], gla_gate_up_fwd[l], gla_gate_b_fwd[l], gla_gate_up_bwd[l],
                        gla_gate_b_bwd[l], gla_norm_g[l], rwkv_mu_prev[l], rwkv_mu_next[l],
                        rwkv_w0_fwd[l], rwkv_w_up_fwd[l], rwkv_w0_bwd[l], rwkv_w_up_bwd[l],
                        rwkv_a0[l], rwkv_a_up[l], rwkv_g_up[l], rwkv_k_k[l], rwkv_k_a[l], rwkv_r_k[l],
                        rwkv_lnx_g[l], rwkv_lnx_b[l], mix_w_out[l])
        x = _layer_norm(alpha * x + tm, ln_mix_g[l], ln_mix_b[l])
        ca = _memory_attn(x, mem, mem_ln_g[l], mem_ln_b[l], ca_w_q[l], ca_w_kv[l], ca_w_o[l])
        x = _layer_norm(alpha * x + ca, ln_ca_g[l], ln_ca_b[l])
        x = _layer_norm(alpha * x + 0.5 * _swiglu(x, ffn2_w_in[l], ffn2_w_out[l]), ln_ffn2_g[l], ln_ffn2_b[l])
        return x

    y_prompt = x_prompt
    y_sample = x_sample
    for l in range(DEPTH):
        y_prompt = layer(y_prompt, mem_prompt, l)
        y_sample = layer(y_sample, mem_sample, l)
    return (y_prompt, y_sample)
```

```python
import functools

import jax
import jax.numpy as jnp
from jax import lax
from jax.experimental import pallas as pl
from jax.experimental.pallas import tpu as pltpu

F32 = jnp.float32
BF16 = jnp.bfloat16
HI = lax.Precision.HIGHEST

D_MODEL = 1024
D_FF = 2816
DEPTH = 1
GLA_HEADS = 4
GLA_DK = 64
GLA_DV = 128
GLA_QK = GLA_HEADS * GLA_DK
GLA_W = GLA_HEADS * GLA_DV
GLA_RANK = 16
GLA_TAU = 16.0
RW_HEADS = 8
RW_N = 64
RW_W = RW_HEADS * RW_N
RW_DECAY_RANK = 64
RW_AAA_RANK = 64
RW_GATE_RANK = 128
RW_GN_EPS = 64e-5
GLA_COLS = 1568
RW_COLS = 1856
MEM_HEADS = 4
MEM_HD = D_MODEL // MEM_HEADS
LN_EPS = 1e-5
ALPHA = (2.0 * DEPTH) ** 0.25
CHUNK = 64

PA_COLS = 1664
PB_COLS = 1920

VMEM_LIMIT = 56 * 1024 * 1024


def _cp(n_axes):
    return pltpu.CompilerParams(dimension_semantics=("arbitrary",) * n_axes,
                                vmem_limit_bytes=VMEM_LIMIT)


def _mm(a, b, dims=((1,), (0,)), prec=None):
    if prec is None:
        a = a.astype(BF16)
        b = b.astype(BF16)
    return lax.dot_general(a, b, (dims, ((), ())), precision=prec, preferred_element_type=F32)


_NN = ((1,), (0,))
_NT = ((1,), (1,))
_TN = ((0,), (0,))


def _bmm(a, b, ca, cb, prec=HI):
    if prec is None:
        a = a.astype(BF16)
        b = b.astype(BF16)
    return lax.dot_general(a, b, (((ca,), (cb,)), ((0,), (0,))), precision=prec,
                           preferred_element_type=F32)


def _layer_norm(z, g, b, eps):
    mu = jnp.mean(z, axis=-1, keepdims=True)
    zc = z - mu
    var = jnp.mean(zc * zc, axis=-1, keepdims=True)
    return zc * lax.rsqrt(var + eps) * g + b


def _softplus(x):
    return jnp.maximum(x, 0.0) + jnp.log1p(jnp.exp(-jnp.abs(x)))


def _const_spec(shape):
    nd = len(shape)
    return pl.BlockSpec(shape, lambda *_: (0,) * nd, pipeline_mode=pl.Buffered(1))


def _ffn_kernel(x_ref, win_ref, wout_ref, g_ref, b_ref, o_ref, *, n_split):
    x = x_ref[...]
    xb = x.astype(BF16)
    fc = D_FF // n_split
    acc = None
    for f in range(n_split):
        gate = jnp.dot(xb, win_ref[:, f * fc:(f + 1) * fc], preferred_element_type=F32)
        up = jnp.dot(xb, win_ref[:, D_FF + f * fc:D_FF + (f + 1) * fc], preferred_element_type=F32)
        h = (gate * jax.nn.sigmoid(gate) * up).astype(BF16)
        y = jnp.dot(h, wout_ref[f * fc:(f + 1) * fc, :], preferred_element_type=F32)
        acc = y if acc is None else acc + y
    o_ref[...] = _layer_norm(ALPHA * x + 0.5 * acc, g_ref[...], b_ref[...], LN_EPS)


def _ffn_block(x, w_in, w_out, g, b, *, tm=512, n_split=2):
    n = x.shape[0]
    return pl.pallas_call(
        functools.partial(_ffn_kernel, n_split=n_split),
        grid=(n // tm,),
        in_specs=[pl.BlockSpec((tm, D_MODEL), lambda i: (i, 0)),
                  _const_spec((D_MODEL, 2 * D_FF)),
                  _const_spec((D_FF, D_MODEL)),
                  _const_spec((1, D_MODEL)),
                  _const_spec((1, D_MODEL))],
        out_specs=pl.BlockSpec((tm, D_MODEL), lambda i: (i, 0)),
        out_shape=jax.ShapeDtypeStruct((n, D_MODEL), F32),
        compiler_params=_cp(1),
        name="ffn_block",
    )(x, w_in, w_out, g, b)


def _inproj_kernel(x_ref, wa_ref, wb_ref, oa_ref, ob_ref):
    xb = x_ref[...].astype(BF16)
    oa_ref[...] = jnp.dot(xb, wa_ref[...], preferred_element_type=F32)
    ob_ref[...] = jnp.dot(xb, wb_ref[...], preferred_element_type=F32)


def _in_proj(x, wa, wb, *, tm=512):
    n = x.shape[0]
    return pl.pallas_call(
        _inproj_kernel,
        grid=(n // tm,),
        in_specs=[pl.BlockSpec((tm, D_MODEL), lambda i: (i, 0)),
                  _const_spec((D_MODEL, PA_COLS)),
                  _const_spec((D_MODEL, PB_COLS))],
        out_specs=[pl.BlockSpec((tm, PA_COLS), lambda i: (i, 0)),
                   pl.BlockSpec((tm, PB_COLS), lambda i: (i, 0))],
        out_shape=[jax.ShapeDtypeStruct((n, PA_COLS), F32),
                   jax.ShapeDtypeStruct((n, PB_COLS), F32)],
        compiler_params=_cp(1),
        name="in_proj",
    )(x, wa, wb)


def _tri_masks(reverse):
    row = lax.broadcasted_iota(jnp.int32, (CHUNK, CHUNK), 0)
    col = lax.broadcasted_iota(jnp.int32, (CHUNK, CHUNK), 1)
    incl = (col >= row) if reverse else (col <= row)
    strict = (col > row) if reverse else (col < row)
    return row, col, incl, strict


def _gla_kernel(q_ref, k_ref, v_ref, gd_ref, up_ref, gb_ref, o_ref, s_ref, *, reverse, tb):
    @pl.when(pl.program_id(1) == 0)
    def _():
        s_ref[...] = jnp.zeros_like(s_ref)

    _, _, incl, _ = _tri_masks(reverse)
    cum = incl.astype(F32)
    n_chunk = tb // CHUNK
    last = 0 if reverse else CHUNK - 1

    def body(ci, carry):
        c = (n_chunk - 1 - ci) if reverse else ci
        sl = pl.ds(pl.multiple_of(c * CHUNK, CHUNK), CHUNK)
        q = q_ref[sl, :] * (GLA_DK ** -0.5)
        k = k_ref[sl, :]
        v = v_ref[sl, :]
        z = _mm(gd_ref[sl, :], up_ref[...], _NN, HI) + gb_ref[...]
        log_a = -_softplus(-z) / GLA_TAU
        b = _mm(cum, log_a, _NN, HI)
        b_last = b[last:last + 1, :]
        b_mid = 0.5 * b_last
        qt = q * jnp.exp(b - b_mid)
        kt = k * jnp.exp(b_mid - b)
        qs = q * jnp.exp(b)
        ks = k * jnp.exp(b_last - b)
        dec = jnp.exp(b_last)
        for h in range(GLA_HEADS):
            hs = slice(h * GLA_DK, (h + 1) * GLA_DK)
            vs = slice(h * GLA_DV, (h + 1) * GLA_DV)
            scores = jnp.where(incl, _mm(qt[:, hs], kt[:, hs], _NT), 0.0)
            st = s_ref[h]
            o_ref[sl, vs] = _mm(scores, v[:, vs], _NN) + _mm(qs[:, hs], st, _NT)
            s_ref[h] = st * dec[:, hs] + _mm(v[:, vs], ks[:, hs], _TN)
        return carry

    lax.fori_loop(0, n_chunk, body, 0)


def _gla_dir(pa, up_pad, gate_b, *, reverse, tb=256):
    bsz, t_len, _ = pa.shape
    nt = t_len // tb
    tmap = (lambda t: nt - 1 - t) if reverse else (lambda t: t)
    return pl.pallas_call(
        functools.partial(_gla_kernel, reverse=reverse, tb=tb),
        grid=(bsz, nt),
        in_specs=[pl.BlockSpec((None, tb, GLA_QK), lambda b, t: (b, tmap(t), 0)),
                  pl.BlockSpec((None, tb, GLA_QK), lambda b, t: (b, tmap(t), 1)),
                  pl.BlockSpec((None, tb, GLA_W), lambda b, t: (b, tmap(t), 1)),
                  pl.BlockSpec((None, tb, 128), lambda b, t: (b, tmap(t), 12)),
                  _const_spec((128, GLA_QK)),
                  _const_spec((1, GLA_QK))],
        out_specs=pl.BlockSpec((None, tb, GLA_W), lambda b, t: (b, tmap(t), 0)),
        out_shape=jax.ShapeDtypeStruct((bsz, t_len, GLA_W), F32),
        scratch_shapes=[pltpu.VMEM((GLA_HEADS, GLA_DV, GLA_DK), F32)],
        compiler_params=_cp(2),
        name="gla_bwd" if reverse else "gla_fwd",
    )(pa, pa, pa, pa, up_pad, gate_b)


def _seg_sum(x, seg):
    hi = x.astype(BF16)
    lo = (x - hi.astype(F32)).astype(BF16)
    return (jnp.dot(hi, seg, preferred_element_type=F32) + jnp.dot(lo, seg, preferred_element_type=F32))


def _prep_kernel(p_ref, hp_ref, hn_ref, mup_ref, mun_ref, w0f_ref, upf_ref, w0b_ref, upb_ref,
                 a0_ref, aup_ref, gup_ref, kk_ref, ka_ref, rk_ref, seg_ref,
                 r_o, k_o, v_o, kn_o, b_o, lwf_o, lwb_o, gate_o, bonus_o, *, tb, nt):
    t = pl.program_id(1)
    p = p_ref[...]
    row = lax.broadcasted_iota(jnp.int32, (tb, 1), 0)
    halo_prev = jnp.where(t > 0, hp_ref[7:8, :], 0.0)
    halo_next = jnp.where(t < nt - 1, hn_ref[0:1, :], 0.0)
    prev = jnp.where(row == 0, halo_prev, pltpu.roll(p, 1, 0))
    nxt = jnp.where(row == tb - 1, halo_next, pltpu.roll(p, tb - 1, 0))
    rw = p + mup_ref[...] * (prev - p) + mun_ref[...] * (nxt - p)

    r = rw[:, 0:512]
    kr = rw[:, 512:1024]
    vr = rw[:, 1024:1536]
    wd = jnp.tanh(rw[:, 1536:1664])
    ad = rw[:, 1664:1792]
    gd = jax.nn.sigmoid(rw[:, 1792:1920])

    w_f = -_softplus(-(w0f_ref[...] + _mm(wd, upf_ref[...]))) - 0.5
    w_b = -_softplus(-(w0b_ref[...] + _mm(wd, upb_ref[...]))) - 0.5
    lw_f = -jnp.exp(w_f)
    lw_b = -jnp.exp(w_b)
    a = jax.nn.sigmoid(a0_ref[...] + _mm(ad, aup_ref[...]))
    gate = _mm(gd, gup_ref[...])
    seg = seg_ref[...]
    kk = kr * kk_ref[...]
    kn = kk / jnp.maximum(jnp.sqrt(_seg_sum(kk * kk, seg)), 1e-12)
    k2 = kr * (1.0 + (a - 1.0) * ka_ref[...])
    bonus = _seg_sum(r * k2 * rk_ref[...], seg) * vr
    outs = ((r_o, r), (k_o, k2), (v_o, vr), (kn_o, kn), (b_o, a * kn), (lwf_o, lw_f), (lwb_o, lw_b),
            (gate_o, gate), (bonus_o, bonus))
    for ref, val in outs:
        for h in range(RW_HEADS):
            ref[h] = val[:, h * RW_N:(h + 1) * RW_N]


def _rwkv_prep(pb, prm, *, tb=256):
    bsz, t_len, _ = pb.shape
    nt = t_len // tb
    hb = tb // 8
    n8 = t_len // 8
    consts = [prm["mu_prev"], prm["mu_next"], prm["w0_f"], prm["up_f"], prm["w0_b"], prm["up_b"],
              prm["a0"], prm["a_up"], prm["g_up"], prm["k_k"], prm["k_a"], prm["r_k"], prm["seg"]]
    out_spec = pl.BlockSpec((None, RW_HEADS, tb, RW_N), lambda b, t: (b, 0, t, 0))
    out_shape = jax.ShapeDtypeStruct((bsz, RW_HEADS, t_len, RW_N), F32)
    return pl.pallas_call(
        functools.partial(_prep_kernel, tb=tb, nt=nt),
        grid=(bsz, nt),
        in_specs=[pl.BlockSpec((None, tb, PB_COLS), lambda b, t: (b, t, 0)),
                  pl.BlockSpec((None, 8, PB_COLS), lambda b, t: (b, jnp.maximum(t * hb - 1, 0), 0)),
                  pl.BlockSpec((None, 8, PB_COLS), lambda b, t: (b, jnp.minimum((t + 1) * hb, n8 - 1), 0))]
                 + [_const_spec(c.shape) for c in consts],
        out_specs=[out_spec] * 9,
        out_shape=[out_shape] * 9,
        compiler_params=_cp(2),
        name="rwkv_prep",
    )(pb, pb, pb, *consts)


def _scan_kernel(r_ref, lw_ref, k_ref, v_ref, kn_ref, b_ref, y_ref, s_ref, *, reverse, tb):
    @pl.when(pl.program_id(1) == 0)
    def _():
        s_ref[...] = jnp.zeros_like(s_ref)

    row, col, incl, strict = _tri_masks(reverse)
    cum = jnp.broadcast_to(incl.astype(F32)[None], (RW_HEADS, CHUNK, CHUNK))
    eye = (row == col).astype(F32)
    level_masks = []
    m = 1
    while m < CHUNK:
        if reverse:
            level_masks.append(((row // m) % 2 == 0) & (col // m == row // m + 1))
        else:
            level_masks.append(((row // m) % 2 == 1) & (col // m == row // m - 1))
        m *= 2
    n_chunk = tb // CHUNK
    last = 0 if reverse else CHUNK - 1

    def body(ci, carry):
        c = (n_chunk - 1 - ci) if reverse else ci
        sl = pl.ds(pl.multiple_of(c * CHUNK, CHUNK), CHUNK)
        rc, lwc, kc, vc, knc, bc = (ref[:, sl, :] for ref in (r_ref, lw_ref, k_ref, v_ref, kn_ref, b_ref))
        s0 = s_ref[...]
        cs = _bmm(cum, lwc, 2, 1)
        e_tot = jnp.exp(cs[:, last:last + 1, :])
        e_neg = jnp.exp(-cs)
        kn_t = knc * jnp.exp(cs - lwc)
        r_t = rc * jnp.exp(cs)
        b_t = bc * e_neg
        k_t = kc * e_neg
        a_b = jnp.where(strict[None], _bmm(kn_t, b_t, 2, 2), 0.0)
        a_k = jnp.where(strict[None], _bmm(kn_t, k_t, 2, 2), 0.0)
        a_rb = jnp.where(incl[None], _bmm(r_t, b_t, 2, 2), 0.0)
        a_rk = jnp.where(incl[None], _bmm(r_t, k_t, 2, 2), 0.0)
        inv = eye[None] - jnp.where(level_masks[0][None], a_b, 0.0)
        for mk in level_masks[1:]:
            inv = inv - _bmm(inv, _bmm(jnp.where(mk[None], a_b, 0.0), inv, 2, 1), 2, 1)
        u = _bmm(inv, _bmm(kn_t, s0, 2, 2) + _bmm(a_k, vc, 2, 1), 2, 1)
        y_ref[:, sl, :] = _bmm(r_t, s0, 2, 2) + _bmm(a_rk, vc, 2, 1) - _bmm(a_rb, u, 2, 1)
        s_ref[...] = s0 * e_tot + _bmm(vc, k_t * e_tot, 1, 1) - _bmm(u, b_t * e_tot, 1, 1)
        return carry

    lax.fori_loop(0, n_chunk, body, 0)


def _rwkv_scan(r, lw, k, v, kn, b, *, reverse, tb=256):
    bsz, n_h, t_len, n = r.shape
    nt = t_len // tb
    tmap = (lambda t: nt - 1 - t) if reverse else (lambda t: t)
    spec = pl.BlockSpec((None, n_h, tb, n), lambda bi, ti: (bi, 0, tmap(ti), 0))
    return pl.pallas_call(
        functools.partial(_scan_kernel, reverse=reverse, tb=tb),
        grid=(bsz, nt),
        in_specs=[spec] * 6,
        out_specs=spec,
        out_shape=jax.ShapeDtypeStruct(r.shape, F32),
        scratch_shapes=[pltpu.VMEM((n_h, n, n), F32)],
        compiler_params=_cp(2),
        name="rwkv_scan_bwd" if reverse else "rwkv_scan_fwd",
    )(r, lw, k, v, kn, b)


def _mixout_kernel(of_ref, ob_ref, g_ref, gn_ref, yf_ref, yb_ref, gate_ref, bonus_ref, lg_ref, lb_ref,
                   x_ref, wg_ref, wr_ref, ng_ref, nb_ref, o_ref):
    o = of_ref[...] + ob_ref[...]
    parts = []
    for h in range(GLA_HEADS):
        oh = o[:, h * GLA_DV:(h + 1) * GLA_DV]
        parts.append(oh * lax.rsqrt(jnp.mean(oh * oh, axis=-1, keepdims=True) + LN_EPS))
    g = g_ref[...]
    o = jnp.concatenate(parts, axis=-1) * gn_ref[...] * (g * jax.nn.sigmoid(g))
    acc = jnp.dot(o.astype(BF16), wg_ref[...], preferred_element_type=F32)
    y = yf_ref[...] + yb_ref[...]
    mu = jnp.mean(y, axis=-1, keepdims=True)
    yc = y - mu
    var = jnp.mean(yc * yc, axis=-1, keepdims=True)
    y = yc * lax.rsqrt(var + RW_GN_EPS) * lg_ref[...] + lb_ref[...]
    y = ((y + bonus_ref[...]) * gate_ref[...]).astype(BF16)
    for h in range(RW_HEADS):
        acc = acc + jnp.dot(y[h], wr_ref[h], preferred_element_type=F32)
    o_ref[...] = _layer_norm(ALPHA * x_ref[...] + acc, ng_ref[...], nb_ref[...], LN_EPS)


def _mix_out(o_f, o_b, pa, gla_norm_g, y_f, y_b, gate, bonus, lnx_g, lnx_b, x, w_gla, w_rw, ln_g, ln_b,
             *, tm=256):
    bsz, t_len, _ = x.shape
    row = lambda width: pl.BlockSpec((None, tm, width), lambda b, t: (b, t, 0))
    hm = pl.BlockSpec((None, RW_HEADS, tm, RW_N), lambda b, t: (b, 0, t, 0))
    return pl.pallas_call(
        _mixout_kernel,
        grid=(bsz, t_len // tm),
        in_specs=[row(GLA_W), row(GLA_W),
                  pl.BlockSpec((None, tm, GLA_W), lambda b, t: (b, t, 2)),
                  _const_spec((1, GLA_W)),
                  hm, hm, hm, hm,
                  _const_spec((RW_HEADS, 1, RW_N)), _const_spec((RW_HEADS, 1, RW_N)),
                  row(D_MODEL),
                  _const_spec((GLA_W, D_MODEL)), _const_spec((RW_HEADS, RW_N, D_MODEL)),
                  _const_spec((1, D_MODEL)), _const_spec((1, D_MODEL))],
        out_specs=row(D_MODEL),
        out_shape=jax.ShapeDtypeStruct(x.shape, F32),
        compiler_params=_cp(2),
        name="mix_out",
    )(o_f, o_b, pa, gla_norm_g, y_f, y_b, gate, bonus, lnx_g, lnx_b, x, w_gla, w_rw, ln_g, ln_b)


def _memkv_kernel(m_ref, g_ref, b_ref, w_ref, k_ref, v_ref):
    m = _layer_norm(m_ref[...], g_ref[...], b_ref[...], LN_EPS)
    kv = jnp.dot(m.astype(BF16), w_ref[...], preferred_element_type=F32)
    k_ref[...] = kv[:, :D_MODEL]
    v_ref[...] = kv[:, D_MODEL:]


def _mem_kv(mem, g, b, w_kv):
    bsz, m_tok, _ = mem.shape
    spec = pl.BlockSpec((None, m_tok, D_MODEL), lambda i: (i, 0, 0))
    shape = jax.ShapeDtypeStruct(mem.shape, F32)
    return pl.pallas_call(
        _memkv_kernel,
        grid=(bsz,),
        in_specs=[spec, _const_spec((1, D_MODEL)), _const_spec((1, D_MODEL)),
                  _const_spec((D_MODEL, 2 * D_MODEL))],
        out_specs=[spec, spec],
        out_shape=[shape, shape],
        compiler_params=_cp(1),
        name="mem_kv",
    )(mem, g, b, w_kv)


def _ca_kernel(x_ref, k_ref, v_ref, wq_ref, wo_ref, g_ref, b_ref, o_ref):
    x = x_ref[...]
    q = jnp.dot(x.astype(BF16), wq_ref[...], preferred_element_type=F32)
    k = k_ref[...]
    v = v_ref[...]
    parts = []
    for h in range(MEM_HEADS):
        hs = slice(h * MEM_HD, (h + 1) * MEM_HD)
        s = _mm(q[:, hs], k[:, hs], _NT) * (MEM_HD ** -0.5)
        e = jnp.exp(s - jnp.max(s, axis=-1, keepdims=True))
        p = e / jnp.sum(e, axis=-1, keepdims=True)
        parts.append(_mm(p, v[:, hs], _NN))
    ca = jnp.dot(jnp.concatenate(parts, axis=-1).astype(BF16), wo_ref[...], preferred_element_type=F32)
    o_ref[...] = _layer_norm(ALPHA * x + ca, g_ref[...], b_ref[...], LN_EPS)


def _cross_attn(x, k, v, w_q, w_o, g, b, *, tm=512):
    bsz, t_len, _ = x.shape
    m_tok = k.shape[1]
    row = pl.BlockSpec((None, tm, D_MODEL), lambda bi, t: (bi, t, 0))
    mem = pl.BlockSpec((None, m_tok, D_MODEL), lambda bi, t: (bi, 0, 0))
    return pl.pallas_call(
        _ca_kernel,
        grid=(bsz, t_len // tm),
        in_specs=[row, mem, mem, _const_spec((D_MODEL, D_MODEL)), _const_spec((D_MODEL, D_MODEL)),
                  _const_spec((1, D_MODEL)), _const_spec((1, D_MODEL))],
        out_specs=row,
        out_shape=jax.ShapeDtypeStruct(x.shape, F32),
        compiler_params=_cp(2),
        name="cross_attn",
    )(x, k, v, w_q, w_o, g, b)


def _pad_rows(w, start, total):
    return jnp.zeros((total, w.shape[1]), w.dtype).at[start:start + w.shape[0]].set(w)


def _layer(x, mem, p):
    bsz, t_len, _ = x.shape
    x2d = x.reshape(bsz * t_len, D_MODEL)
    x1 = _ffn_block(x2d, p["ffn1_w_in"], p["ffn1_w_out"], p["ln_ffn1_g"], p["ln_ffn1_b"])
    pa, pb = _in_proj(x1, p["w_pa"], p["w_pb"])
    pa = pa.reshape(bsz, t_len, PA_COLS)
    pb = pb.reshape(bsz, t_len, PB_COLS)
    o_f = _gla_dir(pa, p["gla_up_f"], p["gla_b_f"], reverse=False)
    o_b = _gla_dir(pa, p["gla_up_b"], p["gla_b_b"], reverse=True)
    r, k2, v, kn, b, lw_f, lw_b, gate, bonus = _rwkv_prep(pb, p["rw"])
    y_f = _rwkv_scan(r, lw_f, k2, v, kn, b, reverse=False)
    y_b = _rwkv_scan(r, lw_b, k2, v, kn, b, reverse=True)
    x2 = _mix_out(o_f, o_b, pa, p["gla_norm_g"], y_f, y_b, gate, bonus, p["lnx_g"], p["lnx_b"],
                  x1.reshape(bsz, t_len, D_MODEL), p["w_out_gla"], p["w_out_rw"], p["ln_mix_g"], p["ln_mix_b"])
    mk, mv = _mem_kv(mem, p["mem_ln_g"], p["mem_ln_b"], p["ca_w_kv"])
    x3 = _cross_attn(x2, mk, mv, p["ca_w_q"], p["ca_w_o"], p["ln_ca_g"], p["ln_ca_b"])
    x4 = _ffn_block(x3.reshape(bsz * t_len, D_MODEL), p["ffn2_w_in"], p["ffn2_w_out"],
                    p["ln_ffn2_g"], p["ln_ffn2_b"])
    return x4.reshape(bsz, t_len, D_MODEL)


def kernel(x_prompt, x_sample, mem_prompt, mem_sample, ffn1_w_in, ffn1_w_out, ln_ffn1_g, ln_ffn1_b, mix_w_in, gla_gate_up_fwd, gla_gate_b_fwd, gla_gate_up_bwd, gla_gate_b_bwd, gla_norm_g, rwkv_mu_prev, rwkv_mu_next, rwkv_w0_fwd, rwkv_w_up_fwd, rwkv_w0_bwd, rwkv_w_up_bwd, rwkv_a0, rwkv_a_up, rwkv_g_up, rwkv_k_k, rwkv_k_a, rwkv_r_k, rwkv_lnx_g, rwkv_lnx_b, mix_w_out, ln_mix_g, ln_mix_b, mem_ln_g, mem_ln_b, ca_w_q, ca_w_kv, ca_w_o, ln_ca_g, ln_ca_b, ffn2_w_in, ffn2_w_out, ln_ffn2_g, ln_ffn2_b):
    y_prompt, y_sample = x_prompt, x_sample
    for l in range(DEPTH):
        row = lambda a: a[l].reshape(1, -1)
        w_in = mix_w_in[l]
        zeros = lambda n: jnp.zeros((D_MODEL, n), F32)
        rw_off = GLA_COLS
        lr_off = rw_off + 3 * RW_W
        gd_off = lr_off + 2 * RW_DECAY_RANK + RW_AAA_RANK
        w_pa = jnp.concatenate([w_in[:, :GLA_COLS], zeros(PA_COLS - GLA_COLS)], axis=1)
        w_pb = jnp.concatenate([w_in[:, rw_off:gd_off], zeros(64), w_in[:, gd_off:]], axis=1)
        perm_mu = lambda mu: jnp.concatenate(
            [mu[:gd_off - rw_off], jnp.zeros((64,), F32), mu[gd_off - rw_off:]]).reshape(1, PB_COLS)
        rw = {
            "mu_prev": perm_mu(rwkv_mu_prev[l]), "mu_next": perm_mu(rwkv_mu_next[l]),
            "w0_f": row(rwkv_w0_fwd), "up_f": _pad_rows(rwkv_w_up_fwd[l], 0, 128).astype(BF16),
            "w0_b": row(rwkv_w0_bwd), "up_b": _pad_rows(rwkv_w_up_bwd[l], RW_DECAY_RANK, 128).astype(BF16),
            "a0": row(rwkv_a0), "a_up": _pad_rows(rwkv_a_up[l], 0, 128).astype(BF16),
            "g_up": rwkv_g_up[l].astype(BF16),
            "k_k": row(rwkv_k_k), "k_a": row(rwkv_k_a), "r_k": rwkv_r_k[l].reshape(1, RW_W),
            "seg": jnp.kron(jnp.eye(RW_HEADS, dtype=F32), jnp.ones((RW_N, RW_N), F32)).astype(BF16),
        }
        p = {
            "ffn1_w_in": ffn1_w_in[l].astype(BF16), "ffn1_w_out": ffn1_w_out[l].astype(BF16),
            "ln_ffn1_g": row(ln_ffn1_g), "ln_ffn1_b": row(ln_ffn1_b),
            "w_pa": w_pa.astype(BF16), "w_pb": w_pb.astype(BF16),
            "gla_up_f": _pad_rows(gla_gate_up_fwd[l], 0, 128), "gla_b_f": row(gla_gate_b_fwd),
            "gla_up_b": _pad_rows(gla_gate_up_bwd[l], GLA_RANK, 128), "gla_b_b": row(gla_gate_b_bwd),
            "gla_norm_g": row(gla_norm_g),
            "rw": rw,
            "lnx_g": rwkv_lnx_g[l].reshape(RW_HEADS, 1, RW_N), "lnx_b": rwkv_lnx_b[l].reshape(RW_HEADS, 1, RW_N),
            "w_out_gla": mix_w_out[l][:GLA_W].astype(BF16),
            "w_out_rw": mix_w_out[l][GLA_W:].reshape(RW_HEADS, RW_N, D_MODEL).astype(BF16),
            "ln_mix_g": row(ln_mix_g), "ln_mix_b": row(ln_mix_b),
            "mem_ln_g": row(mem_ln_g), "mem_ln_b": row(mem_ln_b),
            "ca_w_q": ca_w_q[l].astype(BF16), "ca_w_kv": ca_w_kv[l].astype(BF16), "ca_w_o": ca_w_o[l].astype(BF16),
            "ln_ca_g": row(ln_ca_g), "ln_ca_b": row(ln_ca_b),
            "ffn2_w_in": ffn2_w_in[l].astype(BF16), "ffn2_w_out": ffn2_w_out[l].astype(BF16),
            "ln_ffn2_g": row(ln_ffn2_g), "ln_ffn2_b": row(ln_ffn2_b),
        }
        y_prompt = _layer(y_prompt, mem_prompt, p)
        y_sample = _layer(y_sample, mem_sample, p)
    return (y_prompt, y_sample)
```

```python
import functools

import jax
import jax.numpy as jnp
from jax import lax
from jax.experimental import pallas as pl
from jax.experimental.pallas import tpu as pltpu

F32 = jnp.float32
BF16 = jnp.bfloat16
HI = lax.Precision.HIGHEST

D_MODEL = 1024
D_FF = 2816
DEPTH = 1
GLA_HEADS = 4
GLA_DK = 64
GLA_DV = 128
GLA_QK = GLA_HEADS * GLA_DK
GLA_W = GLA_HEADS * GLA_DV
GLA_RANK = 16
GLA_TAU = 16.0
RW_HEADS = 8
RW_N = 64
RW_W = RW_HEADS * RW_N
RW_DECAY_RANK = 64
RW_AAA_RANK = 64
RW_GATE_RANK = 128
RW_GN_EPS = 64e-5
GLA_COLS = 1568
RW_COLS = 1856
MEM_HEADS = 4
MEM_HD = D_MODEL // MEM_HEADS
LN_EPS = 1e-5
ALPHA = (2.0 * DEPTH) ** 0.25
CHUNK = 64

PA_COLS = 1664
PB_COLS = 1920

VMEM_LIMIT = 56 * 1024 * 1024


def _cp(n_axes):
    return pltpu.CompilerParams(dimension_semantics=("arbitrary",) * n_axes,
                                vmem_limit_bytes=VMEM_LIMIT)


def _mm(a, b, dims=((1,), (0,)), prec=None):
    if prec is None:
        a = a.astype(BF16)
        b = b.astype(BF16)
    return lax.dot_general(a, b, (dims, ((), ())), precision=prec, preferred_element_type=F32)


_NN = ((1,), (0,))
_NT = ((1,), (1,))
_TN = ((0,), (0,))


def _bmm(a, b, ca, cb, prec=None):
    if prec is None:
        a = a.astype(BF16)
        b = b.astype(BF16)
    return lax.dot_general(a, b, (((ca,), (cb,)), ((0,), (0,))), precision=prec,
                           preferred_element_type=F32)


def _layer_norm(z, g, b, eps):
    mu = jnp.mean(z, axis=-1, keepdims=True)
    zc = z - mu
    var = jnp.mean(zc * zc, axis=-1, keepdims=True)
    return zc * lax.rsqrt(var + eps) * g + b


def _softplus(x):
    return jnp.maximum(x, 0.0) + jnp.log1p(jnp.exp(-jnp.abs(x)))


def _const_spec(shape):
    nd = len(shape)
    return pl.BlockSpec(shape, lambda *_: (0,) * nd, pipeline_mode=pl.Buffered(1))


def _ffn_kernel(x_ref, win_ref, wout_ref, g_ref, b_ref, o_ref, *, n_split):
    x = x_ref[...]
    xb = x.astype(BF16)
    fc = D_FF // n_split
    acc = None
    for f in range(n_split):
        gate = jnp.dot(xb, win_ref[:, f * fc:(f + 1) * fc], preferred_element_type=F32)
        up = jnp.dot(xb, win_ref[:, D_FF + f * fc:D_FF + (f + 1) * fc], preferred_element_type=F32)
        h = (gate * jax.nn.sigmoid(gate) * up).astype(BF16)
        y = jnp.dot(h, wout_ref[f * fc:(f + 1) * fc, :], preferred_element_type=F32)
        acc = y if acc is None else acc + y
    o_ref[...] = _layer_norm(ALPHA * x + 0.5 * acc, g_ref[...], b_ref[...], LN_EPS)


def _ffn_block(x, w_in, w_out, g, b, *, tm=512, n_split=2):
    n = x.shape[0]
    return pl.pallas_call(
        functools.partial(_ffn_kernel, n_split=n_split),
        grid=(n // tm,),
        in_specs=[pl.BlockSpec((tm, D_MODEL), lambda i: (i, 0)),
                  _const_spec((D_MODEL, 2 * D_FF)),
                  _const_spec((D_FF, D_MODEL)),
                  _const_spec((1, D_MODEL)),
                  _const_spec((1, D_MODEL))],
        out_specs=pl.BlockSpec((tm, D_MODEL), lambda i: (i, 0)),
        out_shape=jax.ShapeDtypeStruct((n, D_MODEL), F32),
        compiler_params=_cp(1),
        name="ffn_block",
    )(x, w_in, w_out, g, b)


def _inproj_kernel(x_ref, wa_ref, wb_ref, oa_ref, ob_ref):
    xb = x_ref[...].astype(BF16)
    oa_ref[...] = jnp.dot(xb, wa_ref[...], preferred_element_type=F32)
    ob_ref[...] = jnp.dot(xb, wb_ref[...], preferred_element_type=F32)


def _in_proj(x, wa, wb, *, tm=512):
    n = x.shape[0]
    return pl.pallas_call(
        _inproj_kernel,
        grid=(n // tm,),
        in_specs=[pl.BlockSpec((tm, D_MODEL), lambda i: (i, 0)),
                  _const_spec((D_MODEL, PA_COLS)),
                  _const_spec((D_MODEL, PB_COLS))],
        out_specs=[pl.BlockSpec((tm, PA_COLS), lambda i: (i, 0)),
                   pl.BlockSpec((tm, PB_COLS), lambda i: (i, 0))],
        out_shape=[jax.ShapeDtypeStruct((n, PA_COLS), F32),
                   jax.ShapeDtypeStruct((n, PB_COLS), F32)],
        compiler_params=_cp(1),
        name="in_proj",
    )(x, wa, wb)


def _tri_masks(reverse):
    row = lax.broadcasted_iota(jnp.int32, (CHUNK, CHUNK), 0)
    col = lax.broadcasted_iota(jnp.int32, (CHUNK, CHUNK), 1)
    incl = (col >= row) if reverse else (col <= row)
    strict = (col > row) if reverse else (col < row)
    return row, col, incl, strict


def _cum_consts(tb):
    idx = jnp.arange(tb)
    same = (idx[:, None] // CHUNK) == (idx[None, :] // CHUNK)
    fwd = same & (idx[None, :] <= idx[:, None])
    bwd = same & (idx[None, :] >= idx[:, None])
    cat = lambda tri: jnp.concatenate([tri, same], axis=0).astype(BF16)
    return cat(fwd), cat(bwd)


def _mm01(m01, x):
    hi = x.astype(BF16)
    r1 = x - hi.astype(F32)
    mid = r1.astype(BF16)
    lo = (r1 - mid.astype(F32)).astype(BF16)
    dot = lambda p: jnp.dot(m01, p, preferred_element_type=F32)
    return dot(hi) + dot(mid) + dot(lo)


def _gla_kernel(qf_ref, kf_ref, vf_ref, gf_ref, qb_ref, kb_ref, vb_ref, gb_ref,
                upf_ref, bf_ref, upb_ref, bb_ref, cf_ref, cb_ref, of_ref, ob_ref,
                s_ref, qt_s, kt_s, qs_s, ks_s, v_s, dec_s, *, tb):
    @pl.when(pl.program_id(1) == 0)
    def _():
        s_ref[...] = jnp.zeros_like(s_ref)

    streams = ((qf_ref, kf_ref, vf_ref, gf_ref, upf_ref, bf_ref, cf_ref),
               (qb_ref, kb_ref, vb_ref, gb_ref, upb_ref, bb_ref, cb_ref))
    for d, (q_ref, k_ref, v_ref, g_ref, up_ref, gbias_ref, cat_ref) in enumerate(streams):
        z = _mm(g_ref[...], up_ref[...], _NN, HI) + gbias_ref[...]
        log_a = -_softplus(-z) / GLA_TAU
        ct = _mm01(cat_ref[...], log_a)
        b = ct[:tb]
        tot = ct[tb:]
        q = q_ref[...] * (GLA_DK ** -0.5)
        k = k_ref[...]
        qt_s[d] = (q * jnp.exp(b - 0.5 * tot)).astype(BF16)
        kt_s[d] = (k * jnp.exp(0.5 * tot - b)).astype(BF16)
        qs_s[d] = (q * jnp.exp(b)).astype(BF16)
        ks_s[d] = (k * jnp.exp(tot - b)).astype(BF16)
        dec_s[d] = jnp.exp(tot)
        v_s[d] = v_ref[...].astype(BF16)

    incl = (_tri_masks(False)[2], _tri_masks(True)[2])
    outs = (of_ref, ob_ref)
    n_chunk = tb // CHUNK

    def body(ci, carry):
        work = []
        for d in range(2):
            c = ci if d == 0 else n_chunk - 1 - ci
            start = pl.multiple_of(c * CHUNK, CHUNK)
            sl = pl.ds(start, CHUNK)
            qt, kt, qs, ks, v = (s[d, sl, :] for s in (qt_s, kt_s, qs_s, ks_s, v_s))
            dec = dec_s[d, pl.ds(start, 1), :]
            for h in range(GLA_HEADS):
                hs = slice(h * GLA_DK, (h + 1) * GLA_DK)
                vs = slice(h * GLA_DV, (h + 1) * GLA_DV)
                work.append((d, h, sl, vs, qt[:, hs], kt[:, hs], qs[:, hs], ks[:, hs], v[:, vs], dec[:, hs],
                             s_ref[d, h]))
        scores = [jnp.where(incl[w[0]], _mm(w[4], w[5], _NT), 0.0) for w in work]
        inter = [_mm(w[6], w[10], _NT) for w in work]
        kv = [_mm(w[8], w[7], _TN) for w in work]
        intra = [_mm(sc, w[8], _NN) for sc, w in zip(scores, work)]
        for w, o_inter, o_intra, kv_n in zip(work, inter, intra, kv):
            d, h, sl, vs = w[:4]
            outs[d][sl, vs] = o_intra + o_inter
            s_ref[d, h] = w[10] * w[9] + kv_n
        return carry

    lax.fori_loop(0, n_chunk, body, 0)


def _gla(pa, up_f, bias_f, up_b, bias_b, *, tb=256):
    bsz, t_len, _ = pa.shape
    nt = t_len // tb
    cat_f, cat_b = _cum_consts(tb)

    def stream(tmap):
        return [pl.BlockSpec((None, tb, GLA_QK), lambda b, t: (b, tmap(t), 0)),
                pl.BlockSpec((None, tb, GLA_QK), lambda b, t: (b, tmap(t), 1)),
                pl.BlockSpec((None, tb, GLA_W), lambda b, t: (b, tmap(t), 1)),
                pl.BlockSpec((None, tb, 128), lambda b, t: (b, tmap(t), 12))]

    fwd = lambda t: t
    bwd = lambda t: nt - 1 - t
    out_shape = jax.ShapeDtypeStruct((bsz, t_len, GLA_W), F32)
    return pl.pallas_call(
        functools.partial(_gla_kernel, tb=tb),
        grid=(bsz, nt),
        in_specs=stream(fwd) + stream(bwd)
                 + [_const_spec((128, GLA_QK)), _const_spec((1, GLA_QK)),
                    _const_spec((128, GLA_QK)), _const_spec((1, GLA_QK)),
                    _const_spec((2 * tb, tb)), _const_spec((2 * tb, tb))],
        out_specs=[pl.BlockSpec((None, tb, GLA_W), lambda b, t: (b, fwd(t), 0)),
                   pl.BlockSpec((None, tb, GLA_W), lambda b, t: (b, bwd(t), 0))],
        out_shape=[out_shape, out_shape],
        scratch_shapes=[pltpu.VMEM((2, GLA_HEADS, GLA_DV, GLA_DK), F32)]
                       + [pltpu.VMEM((2, tb, GLA_QK), BF16)] * 4
                       + [pltpu.VMEM((2, tb, GLA_W), BF16), pltpu.VMEM((2, tb, GLA_QK), F32)],
        compiler_params=_cp(2),
        name="gla",
    )(pa, pa, pa, pa, pa, pa, pa, pa, up_f, bias_f, up_b, bias_b, cat_f, cat_b)


def _seg_sum(x, seg):
    hi = x.astype(BF16)
    lo = (x - hi.astype(F32)).astype(BF16)
    return (jnp.dot(hi, seg, preferred_element_type=F32) + jnp.dot(lo, seg, preferred_element_type=F32))


def _prep_kernel(p_ref, hp_ref, hn_ref, mup_ref, mun_ref, w0f_ref, upf_ref, w0b_ref, upb_ref,
                 a0_ref, aup_ref, gup_ref, kk_ref, ka_ref, rk_ref, seg_ref, cf_ref, cb_ref,
                 v_o, gate_o, bonus_o, knf_o, rf_o, bf_o, kf_o, etf_o, knb_o, rb_o, bb_o, kb_o, etb_o,
                 *, tb, nt):
    t = pl.program_id(1)
    p = p_ref[...]
    row = lax.broadcasted_iota(jnp.int32, (tb, 1), 0)
    halo_prev = jnp.where(t > 0, hp_ref[7:8, :], 0.0)
    halo_next = jnp.where(t < nt - 1, hn_ref[0:1, :], 0.0)
    prev = jnp.where(row == 0, halo_prev, pltpu.roll(p, 1, 0))
    nxt = jnp.where(row == tb - 1, halo_next, pltpu.roll(p, tb - 1, 0))
    rw = p + mup_ref[...] * (prev - p) + mun_ref[...] * (nxt - p)

    r = rw[:, 0:512]
    kr = rw[:, 512:1024]
    vr = rw[:, 1024:1536]
    wd = jnp.tanh(rw[:, 1536:1664])
    ad = rw[:, 1664:1792]
    gd = jax.nn.sigmoid(rw[:, 1792:1920])

    a = jax.nn.sigmoid(a0_ref[...] + _mm(ad, aup_ref[...]))
    gate = _mm(gd, gup_ref[...])
    seg = seg_ref[...]
    kk = kr * kk_ref[...]
    kn = kk / jnp.maximum(jnp.sqrt(_seg_sum(kk * kk, seg)), 1e-12)
    k2 = kr * (1.0 + (a - 1.0) * ka_ref[...])
    bonus = _seg_sum(r * k2 * rk_ref[...], seg) * vr
    b = a * kn

    def put(ref, val):
        for h in range(RW_HEADS):
            ref[h] = val[:, h * RW_N:(h + 1) * RW_N]

    put(v_o, vr.astype(BF16))
    put(gate_o, gate)
    put(bonus_o, bonus)
    dirs = ((w0f_ref, upf_ref, cf_ref, knf_o, rf_o, bf_o, kf_o, etf_o),
            (w0b_ref, upb_ref, cb_ref, knb_o, rb_o, bb_o, kb_o, etb_o))
    for w0_ref, up_ref, cat_ref, kn_o, r_o, b_o, k_o, et_o in dirs:
        w = -_softplus(-(w0_ref[...] + _mm(wd, up_ref[...]))) - 0.5
        lw = -jnp.exp(w)
        ct = _mm01(cat_ref[...], lw)
        cs = ct[:tb]
        e_neg = jnp.exp(-cs)
        put(kn_o, (kn * jnp.exp(cs - lw)).astype(BF16))
        put(r_o, (r * jnp.exp(cs)).astype(BF16))
        put(b_o, (b * e_neg).astype(BF16))
        put(k_o, (k2 * e_neg).astype(BF16))
        e_tot = jnp.exp(ct[tb:])
        for c in range(tb // CHUNK):
            for h in range(RW_HEADS):
                et_o[c, h] = e_tot[c * CHUNK:c * CHUNK + 1, h * RW_N:(h + 1) * RW_N]


def _rwkv_prep(pb, prm, *, tb=256):
    bsz, t_len, _ = pb.shape
    nt = t_len // tb
    hb = tb // 8
    n8 = t_len // 8
    n_chunk = tb // CHUNK
    cat_f, cat_b = _cum_consts(tb)
    consts = [prm["mu_prev"], prm["mu_next"], prm["w0_f"], prm["up_f"], prm["w0_b"], prm["up_b"],
              prm["a0"], prm["a_up"], prm["g_up"], prm["k_k"], prm["k_a"], prm["r_k"], prm["seg"],
              cat_f, cat_b]
    hm_spec = pl.BlockSpec((None, RW_HEADS, tb, RW_N), lambda b, t: (b, 0, t, 0))
    hm = lambda dt: jax.ShapeDtypeStruct((bsz, RW_HEADS, t_len, RW_N), dt)
    et_spec = pl.BlockSpec((None, None, n_chunk, RW_HEADS, 1, RW_N), lambda b, t: (b, t, 0, 0, 0, 0))
    et = jax.ShapeDtypeStruct((bsz, nt, n_chunk, RW_HEADS, 1, RW_N), F32)
    per_dir_specs = [hm_spec] * 4 + [et_spec]
    per_dir_shapes = [hm(BF16)] * 4 + [et]
    return pl.pallas_call(
        functools.partial(_prep_kernel, tb=tb, nt=nt),
        grid=(bsz, nt),
        in_specs=[pl.BlockSpec((None, tb, PB_COLS), lambda b, t: (b, t, 0)),
                  pl.BlockSpec((None, 8, PB_COLS), lambda b, t: (b, jnp.maximum(t * hb - 1, 0), 0)),
                  pl.BlockSpec((None, 8, PB_COLS), lambda b, t: (b, jnp.minimum((t + 1) * hb, n8 - 1), 0))]
                 + [_const_spec(c.shape) for c in consts],
        out_specs=[hm_spec] * 3 + per_dir_specs * 2,
        out_shape=[hm(BF16), hm(F32), hm(F32)] + per_dir_shapes * 2,
        compiler_params=_cp(2),
        name="rwkv_prep",
    )(pb, pb, pb, *consts)


def _level_masks(reverse):
    row, col, _, _ = _tri_masks(reverse)
    masks = []
    m = 1
    while m < CHUNK:
        if reverse:
            masks.append(((row // m) % 2 == 0) & (col // m == row // m + 1))
        else:
            masks.append(((row // m) % 2 == 1) & (col // m == row // m - 1))
        m *= 2
    return masks


def _dir_where(mask_f, mask_b, x):
    h = x.shape[0] // 2
    return jnp.concatenate([jnp.where(mask_f[None], x[:h], 0.0), jnp.where(mask_b[None], x[h:], 0.0)], axis=0)


def _chunk_step(kn, r, b, k, v, e_tot, s0):
    row, col, incl_f, strict_f = _tri_masks(False)
    _, _, incl_b, strict_b = _tri_masks(True)
    mask2_f = jnp.concatenate([strict_f, incl_f], axis=0)
    mask2_b = jnp.concatenate([strict_b, incl_b], axis=0)
    eye = (row == col).astype(F32)
    xr = jnp.concatenate([kn, r], axis=1)
    ab = _dir_where(mask2_f, mask2_b, _bmm(xr, b, 2, 2))
    ak = _dir_where(mask2_f, mask2_b, _bmm(xr, k, 2, 2))
    xs = _bmm(xr, s0, 2, 2)
    av = _bmm(ak, v, 2, 1)
    a_b = ab[:, :CHUNK]
    masks_f = _level_masks(False)
    masks_b = _level_masks(True)
    inv = eye[None] - _dir_where(masks_f[0], masks_b[0], a_b)
    for mk_f, mk_b in zip(masks_f[1:], masks_b[1:]):
        inv = inv - _bmm(inv, _bmm(_dir_where(mk_f, mk_b, a_b), inv, 2, 1), 2, 1)
    u = _bmm(inv, xs[:, :CHUNK] + av[:, :CHUNK], 2, 1)
    y = xs[:, CHUNK:] + av[:, CHUNK:] - _bmm(ab[:, CHUNK:], u, 2, 1)
    vu = jnp.concatenate([v, u.astype(BF16)], axis=1)
    kb = jnp.concatenate([k, -b], axis=1)
    s_new = (s0 + _bmm(vu, kb, 1, 1)) * e_tot
    return y, s_new


def _scan_kernel(knf_ref, rf_ref, bf_ref, kf_ref, vf_ref, etf_ref,
                 knb_ref, rb_ref, bb_ref, kb_ref, vb_ref, etb_ref, yf_ref, yb_ref, s_ref, *, tb):
    @pl.when(pl.program_id(1) == 0)
    def _():
        s_ref[...] = jnp.zeros_like(s_ref)

    streams = ((knf_ref, rf_ref, bf_ref, kf_ref, vf_ref), (knb_ref, rb_ref, bb_ref, kb_ref, vb_ref))
    n_chunk = tb // CHUNK
    n_h = yf_ref.shape[0]

    def body(ci, carry):
        cf = ci
        cb = n_chunk - 1 - ci
        sl_f = pl.ds(pl.multiple_of(cf * CHUNK, CHUNK), CHUNK)
        sl_b = pl.ds(pl.multiple_of(cb * CHUNK, CHUNK), CHUNK)
        ops = [jnp.concatenate([streams[0][i][:, sl_f, :], streams[1][i][:, sl_b, :]], axis=0)
               for i in range(5)]
        e_tot = jnp.concatenate([etf_ref[cf], etb_ref[cb]], axis=0)
        y, s_new = _chunk_step(*ops, e_tot, s_ref[...])
        yf_ref[:, sl_f, :] = y[:n_h]
        yb_ref[:, sl_b, :] = y[n_h:]
        s_ref[...] = s_new
        return carry

    lax.fori_loop(0, n_chunk, body, 0)


def _rwkv_scan(v, fwd_ops, bwd_ops, *, tb=256):
    bsz, n_h, t_len, n = v.shape
    nt = t_len // tb
    n_chunk = tb // CHUNK

    def stream(tmap):
        hm = pl.BlockSpec((None, n_h, tb, n), lambda bi, ti: (bi, 0, tmap(ti), 0))
        et = pl.BlockSpec((None, None, n_chunk, n_h, 1, n), lambda bi, ti: (bi, tmap(ti), 0, 0, 0, 0))
        return hm, [hm] * 5 + [et]

    hm_f, specs_f = stream(lambda t: t)
    hm_b, specs_b = stream(lambda t: nt - 1 - t)
    out_shape = jax.ShapeDtypeStruct(v.shape, F32)
    kn_f, r_f, b_f, k_f, et_f = fwd_ops
    kn_b, r_b, b_b, k_b, et_b = bwd_ops
    return pl.pallas_call(
        functools.partial(_scan_kernel, tb=tb),
        grid=(bsz, nt),
        in_specs=specs_f + specs_b,
        out_specs=[hm_f, hm_b],
        out_shape=[out_shape, out_shape],
        scratch_shapes=[pltpu.VMEM((2 * n_h, n, n), F32)],
        compiler_params=_cp(2),
        name="rwkv_scan",
    )(kn_f, r_f, b_f, k_f, v, et_f, kn_b, r_b, b_b, k_b, v, et_b)


def _mixout_kernel(of_ref, ob_ref, g_ref, gn_ref, yf_ref, yb_ref, gate_ref, bonus_ref, lg_ref, lb_ref,
                   x_ref, wg_ref, wr_ref, ng_ref, nb_ref, o_ref):
    o = of_ref[...] + ob_ref[...]
    parts = []
    for h in range(GLA_HEADS):
        oh = o[:, h * GLA_DV:(h + 1) * GLA_DV]
        parts.append(oh * lax.rsqrt(jnp.mean(oh * oh, axis=-1, keepdims=True) + LN_EPS))
    g = g_ref[...]
    o = jnp.concatenate(parts, axis=-1) * gn_ref[...] * (g * jax.nn.sigmoid(g))
    acc = jnp.dot(o.astype(BF16), wg_ref[...], preferred_element_type=F32)
    y = yf_ref[...] + yb_ref[...]
    mu = jnp.mean(y, axis=-1, keepdims=True)
    yc = y - mu
    var = jnp.mean(yc * yc, axis=-1, keepdims=True)
    y = yc * lax.rsqrt(var + RW_GN_EPS) * lg_ref[...] + lb_ref[...]
    y = ((y + bonus_ref[...]) * gate_ref[...]).astype(BF16)
    for h in range(RW_HEADS):
        acc = acc + jnp.dot(y[h], wr_ref[h], preferred_element_type=F32)
    o_ref[...] = _layer_norm(ALPHA * x_ref[...] + acc, ng_ref[...], nb_ref[...], LN_EPS)


def _mix_out(o_f, o_b, pa, gla_norm_g, y_f, y_b, gate, bonus, lnx_g, lnx_b, x, w_gla, w_rw, ln_g, ln_b,
             *, tm=256):
    bsz, t_len, _ = x.shape
    row = lambda width: pl.BlockSpec((None, tm, width), lambda b, t: (b, t, 0))
    hm = pl.BlockSpec((None, RW_HEADS, tm, RW_N), lambda b, t: (b, 0, t, 0))
    return pl.pallas_call(
        _mixout_kernel,
        grid=(bsz, t_len // tm),
        in_specs=[row(GLA_W), row(GLA_W),
                  pl.BlockSpec((None, tm, GLA_W), lambda b, t: (b, t, 2)),
                  _const_spec((1, GLA_W)),
                  hm, hm, hm, hm,
                  _const_spec((RW_HEADS, 1, RW_N)), _const_spec((RW_HEADS, 1, RW_N)),
                  row(D_MODEL),
                  _const_spec((GLA_W, D_MODEL)), _const_spec((RW_HEADS, RW_N, D_MODEL)),
                  _const_spec((1, D_MODEL)), _const_spec((1, D_MODEL))],
        out_specs=row(D_MODEL),
        out_shape=jax.ShapeDtypeStruct(x.shape, F32),
        compiler_params=_cp(2),
        name="mix_out",
    )(o_f, o_b, pa, gla_norm_g, y_f, y_b, gate, bonus, lnx_g, lnx_b, x, w_gla, w_rw, ln_g, ln_b)


def _memkv_kernel(m_ref, g_ref, b_ref, w_ref, k_ref, v_ref):
    m = _layer_norm(m_ref[...], g_ref[...], b_ref[...], LN_EPS)
    kv = jnp.dot(m.astype(BF16), w_ref[...], preferred_element_type=F32)
    k_ref[...] = kv[:, :D_MODEL]
    v_ref[...] = kv[:, D_MODEL:]


def _mem_kv(mem, g, b, w_kv):
    bsz, m_tok, _ = mem.shape
    spec = pl.BlockSpec((None, m_tok, D_MODEL), lambda i: (i, 0, 0))
    shape = jax.ShapeDtypeStruct(mem.shape, F32)
    return pl.pallas_call(
        _memkv_kernel,
        grid=(bsz,),
        in_specs=[spec, _const_spec((1, D_MODEL)), _const_spec((1, D_MODEL)),
                  _const_spec((D_MODEL, 2 * D_MODEL))],
        out_specs=[spec, spec],
        out_shape=[shape, shape],
        compiler_params=_cp(1),
        name="mem_kv",
    )(mem, g, b, w_kv)


def _ca_kernel(x_ref, k_ref, v_ref, wq_ref, wo_ref, g_ref, b_ref, o_ref):
    x = x_ref[...]
    q = jnp.dot(x.astype(BF16), wq_ref[...], preferred_element_type=F32)
    k = k_ref[...]
    v = v_ref[...]
    parts = []
    for h in range(MEM_HEADS):
        hs = slice(h * MEM_HD, (h + 1) * MEM_HD)
        s = _mm(q[:, hs], k[:, hs], _NT) * (MEM_HD ** -0.5)
        e = jnp.exp(s - jnp.max(s, axis=-1, keepdims=True))
        p = e / jnp.sum(e, axis=-1, keepdims=True)
        parts.append(_mm(p, v[:, hs], _NN))
    ca = jnp.dot(jnp.concatenate(parts, axis=-1).astype(BF16), wo_ref[...], preferred_element_type=F32)
    o_ref[...] = _layer_norm(ALPHA * x + ca, g_ref[...], b_ref[...], LN_EPS)


def _cross_attn(x, k, v, w_q, w_o, g, b, *, tm=512):
    bsz, t_len, _ = x.shape
    m_tok = k.shape[1]
    row = pl.BlockSpec((None, tm, D_MODEL), lambda bi, t: (bi, t, 0))
    mem = pl.BlockSpec((None, m_tok, D_MODEL), lambda bi, t: (bi, 0, 0))
    return pl.pallas_call(
        _ca_kernel,
        grid=(bsz, t_len // tm),
        in_specs=[row, mem, mem, _const_spec((D_MODEL, D_MODEL)), _const_spec((D_MODEL, D_MODEL)),
                  _const_spec((1, D_MODEL)), _const_spec((1, D_MODEL))],
        out_specs=row,
        out_shape=jax.ShapeDtypeStruct(x.shape, F32),
        compiler_params=_cp(2),
        name="cross_attn",
    )(x, k, v, w_q, w_o, g, b)


def _pad_rows(w, start, total):
    return jnp.zeros((total, w.shape[1]), w.dtype).at[start:start + w.shape[0]].set(w)


def _layer(x, mem, p):
    bsz, t_len, _ = x.shape
    x2d = x.reshape(bsz * t_len, D_MODEL)
    x1 = _ffn_block(x2d, p["ffn1_w_in"], p["ffn1_w_out"], p["ln_ffn1_g"], p["ln_ffn1_b"])
    pa, pb = _in_proj(x1, p["w_pa"], p["w_pb"])
    pa = pa.reshape(bsz, t_len, PA_COLS)
    pb = pb.reshape(bsz, t_len, PB_COLS)
    o_f, o_b = _gla(pa, p["gla_up_f"], p["gla_b_f"], p["gla_up_b"], p["gla_b_b"])
    v, gate, bonus, *ops = _rwkv_prep(pb, p["rw"])
    y_f, y_b = _rwkv_scan(v, ops[:5], ops[5:])
    x2 = _mix_out(o_f, o_b, pa, p["gla_norm_g"], y_f, y_b, gate, bonus, p["lnx_g"], p["lnx_b"],
                  x1.reshape(bsz, t_len, D_MODEL), p["w_out_gla"], p["w_out_rw"], p["ln_mix_g"], p["ln_mix_b"])
    mk, mv = _mem_kv(mem, p["mem_ln_g"], p["mem_ln_b"], p["ca_w_kv"])
    x3 = _cross_attn(x2, mk, mv, p["ca_w_q"], p["ca_w_o"], p["ln_ca_g"], p["ln_ca_b"])
    x4 = _ffn_block(x3.reshape(bsz * t_len, D_MODEL), p["ffn2_w_in"], p["ffn2_w_out"],
                    p["ln_ffn2_g"], p["ln_ffn2_b"])
    return x4.reshape(bsz, t_len, D_MODEL)


def kernel(x_prompt, x_sample, mem_prompt, mem_sample, ffn1_w_in, ffn1_w_out, ln_ffn1_g, ln_ffn1_b, mix_w_in, gla_gate_up_fwd, gla_gate_b_fwd, gla_gate_up_bwd, gla_gate_b_bwd, gla_norm_g, rwkv_mu_prev, rwkv_mu_next, rwkv_w0_fwd, rwkv_w_up_fwd, rwkv_w0_bwd, rwkv_w_up_bwd, rwkv_a0, rwkv_a_up, rwkv_g_up, rwkv_k_k, rwkv_k_a, rwkv_r_k, rwkv_lnx_g, rwkv_lnx_b, mix_w_out, ln_mix_g, ln_mix_b, mem_ln_g, mem_ln_b, ca_w_q, ca_w_kv, ca_w_o, ln_ca_g, ln_ca_b, ffn2_w_in, ffn2_w_out, ln_ffn2_g, ln_ffn2_b):
    y_prompt, y_sample = x_prompt, x_sample
    for l in range(DEPTH):
        row = lambda a: a[l].reshape(1, -1)
        w_in = mix_w_in[l]
        zeros = lambda n: jnp.zeros((D_MODEL, n), F32)
        rw_off = GLA_COLS
        lr_off = rw_off + 3 * RW_W
        gd_off = lr_off + 2 * RW_DECAY_RANK + RW_AAA_RANK
        w_pa = jnp.concatenate([w_in[:, :GLA_COLS], zeros(PA_COLS - GLA_COLS)], axis=1)
        w_pb = jnp.concatenate([w_in[:, rw_off:gd_off], zeros(64), w_in[:, gd_off:]], axis=1)
        perm_mu = lambda mu: jnp.concatenate(
            [mu[:gd_off - rw_off], jnp.zeros((64,), F32), mu[gd_off - rw_off:]]).reshape(1, PB_COLS)
        rw = {
            "mu_prev": perm_mu(rwkv_mu_prev[l]), "mu_next": perm_mu(rwkv_mu_next[l]),
            "w0_f": row(rwkv_w0_fwd), "up_f": _pad_rows(rwkv_w_up_fwd[l], 0, 128).astype(BF16),
            "w0_b": row(rwkv_w0_bwd), "up_b": _pad_rows(rwkv_w_up_bwd[l], RW_DECAY_RANK, 128).astype(BF16),
            "a0": row(rwkv_a0), "a_up": _pad_rows(rwkv_a_up[l], 0, 128).astype(BF16),
            "g_up": rwkv_g_up[l].astype(BF16),
            "k_k": row(rwkv_k_k), "k_a": row(rwkv_k_a), "r_k": rwkv_r_k[l].reshape(1, RW_W),
            "seg": jnp.kron(jnp.eye(RW_HEADS, dtype=F32), jnp.ones((RW_N, RW_N), F32)).astype(BF16),
        }
        p = {
            "ffn1_w_in": ffn1_w_in[l].astype(BF16), "ffn1_w_out": ffn1_w_out[l].astype(BF16),
            "ln_ffn1_g": row(ln_ffn1_g), "ln_ffn1_b": row(ln_ffn1_b),
            "w_pa": w_pa.astype(BF16), "w_pb": w_pb.astype(BF16),
            "gla_up_f": _pad_rows(gla_gate_up_fwd[l], 0, 128), "gla_b_f": row(gla_gate_b_fwd),
            "gla_up_b": _pad_rows(gla_gate_up_bwd[l], GLA_RANK, 128), "gla_b_b": row(gla_gate_b_bwd),
            "gla_norm_g": row(gla_norm_g),
            "rw": rw,
            "lnx_g": rwkv_lnx_g[l].reshape(RW_HEADS, 1, RW_N), "lnx_b": rwkv_lnx_b[l].reshape(RW_HEADS, 1, RW_N),
            "w_out_gla": mix_w_out[l][:GLA_W].astype(BF16),
            "w_out_rw": mix_w_out[l][GLA_W:].reshape(RW_HEADS, RW_N, D_MODEL).astype(BF16),
            "ln_mix_g": row(ln_mix_g), "ln_mix_b": row(ln_mix_b),
            "mem_ln_g": row(mem_ln_g), "mem_ln_b": row(mem_ln_b),
            "ca_w_q": ca_w_q[l].astype(BF16), "ca_w_kv": ca_w_kv[l].astype(BF16), "ca_w_o": ca_w_o[l].astype(BF16),
            "ln_ca_g": row(ln_ca_g), "ln_ca_b": row(ln_ca_b),
            "ffn2_w_in": ffn2_w_in[l].astype(BF16), "ffn2_w_out": ffn2_w_out[l].astype(BF16),
            "ln_ffn2_g": row(ln_ffn2_g), "ln_ffn2_b": row(ln_ffn2_b),
        }
        y_prompt = _layer(y_prompt, mem_prompt, p)
        y_sample = _layer(y_sample, mem_sample, p)
    return (y_prompt, y_sample)
```

```python
import functools

import jax
import jax.numpy as jnp
from jax import lax
from jax.experimental import pallas as pl
from jax.experimental.pallas import tpu as pltpu

F32 = jnp.float32
BF16 = jnp.bfloat16

D_MODEL = 1024
D_FF = 2816
DEPTH = 1
GLA_HEADS = 4
GLA_DK = 64
GLA_DV = 128
GLA_QK = GLA_HEADS * GLA_DK
GLA_W = GLA_HEADS * GLA_DV
GLA_RANK = 16
GLA_TAU = 16.0
RW_HEADS = 8
RW_N = 64
RW_W = RW_HEADS * RW_N
RW_DECAY_RANK = 64
RW_AAA_RANK = 64
RW_GATE_RANK = 128
RW_GN_EPS = 64e-5
GLA_COLS = 1568
RW_COLS = 1856
MEM_HEADS = 4
MEM_HD = D_MODEL // MEM_HEADS
LN_EPS = 1e-5
ALPHA = (2.0 * DEPTH) ** 0.25
CHUNK = 64
PAIR = 2 * RW_N
N_PAIR = RW_HEADS // 2

PA_COLS = 1664
PB_COLS = 1920

VMEM_LIMIT = 56 * 1024 * 1024


def _cp(n_axes):
    return pltpu.CompilerParams(dimension_semantics=("arbitrary",) * n_axes,
                                vmem_limit_bytes=VMEM_LIMIT)


def _mm(a, b, dims=((1,), (0,))):
    return lax.dot_general(a.astype(BF16), b.astype(BF16), (dims, ((), ())), preferred_element_type=F32)


_NN = ((1,), (0,))
_NT = ((1,), (1,))
_TN = ((0,), (0,))


def _bmm(a, b, ca, cb):
    return lax.dot_general(a.astype(BF16), b.astype(BF16), (((ca,), (cb,)), ((0,), (0,))),
                           preferred_element_type=F32)


def _layer_norm(z, g, b, eps):
    mu = jnp.mean(z, axis=-1, keepdims=True)
    zc = z - mu
    var = jnp.mean(zc * zc, axis=-1, keepdims=True)
    return zc * lax.rsqrt(var + eps) * g + b


def _softplus(x):
    return jnp.maximum(x, 0.0) + jnp.log(1.0 + jnp.exp(-jnp.abs(x)))


def _const_spec(shape):
    nd = len(shape)
    return pl.BlockSpec(shape, lambda *_: (0,) * nd, pipeline_mode=pl.Buffered(1))


def _ffn_kernel(x_ref, win_ref, wout_ref, g_ref, b_ref, o_ref, *, n_split):
    x = x_ref[...]
    xb = x.astype(BF16)
    fc = D_FF // n_split
    acc = None
    for f in range(n_split):
        gate = jnp.dot(xb, win_ref[:, f * fc:(f + 1) * fc], preferred_element_type=F32)
        up = jnp.dot(xb, win_ref[:, D_FF + f * fc:D_FF + (f + 1) * fc], preferred_element_type=F32)
        h = (gate * jax.nn.sigmoid(gate) * up).astype(BF16)
        y = jnp.dot(h, wout_ref[f * fc:(f + 1) * fc, :], preferred_element_type=F32)
        acc = y if acc is None else acc + y
    o_ref[...] = _layer_norm(ALPHA * x + 0.5 * acc, g_ref[...], b_ref[...], LN_EPS)


def _ffn_block(x, w_in, w_out, g, b, *, tm=512, n_split=2):
    n = x.shape[0]
    return pl.pallas_call(
        functools.partial(_ffn_kernel, n_split=n_split),
        grid=(n // tm,),
        in_specs=[pl.BlockSpec((tm, D_MODEL), lambda i: (i, 0)),
                  _const_spec((D_MODEL, 2 * D_FF)),
                  _const_spec((D_FF, D_MODEL)),
                  _const_spec((1, D_MODEL)),
                  _const_spec((1, D_MODEL))],
        out_specs=pl.BlockSpec((tm, D_MODEL), lambda i: (i, 0)),
        out_shape=jax.ShapeDtypeStruct((n, D_MODEL), F32),
        compiler_params=_cp(1),
        name="ffn_block",
    )(x, w_in, w_out, g, b)


def _inproj_kernel(x_ref, wa_ref, wb_ref, oa_ref, ob_ref):
    xb = x_ref[...].astype(BF16)
    oa_ref[...] = jnp.dot(xb, wa_ref[...], preferred_element_type=F32)
    ob_ref[...] = jnp.dot(xb, wb_ref[...], preferred_element_type=F32)


def _in_proj(x, wa, wb, *, tm=512):
    n = x.shape[0]
    return pl.pallas_call(
        _inproj_kernel,
        grid=(n // tm,),
        in_specs=[pl.BlockSpec((tm, D_MODEL), lambda i: (i, 0)),
                  _const_spec((D_MODEL, PA_COLS)),
                  _const_spec((D_MODEL, PB_COLS))],
        out_specs=[pl.BlockSpec((tm, PA_COLS), lambda i: (i, 0)),
                   pl.BlockSpec((tm, PB_COLS), lambda i: (i, 0))],
        out_shape=[jax.ShapeDtypeStruct((n, PA_COLS), F32),
                   jax.ShapeDtypeStruct((n, PB_COLS), F32)],
        compiler_params=_cp(1),
        name="in_proj",
    )(x, wa, wb)


def _tri_masks(reverse):
    row = lax.broadcasted_iota(jnp.int32, (CHUNK, CHUNK), 0)
    col = lax.broadcasted_iota(jnp.int32, (CHUNK, CHUNK), 1)
    incl = (col >= row) if reverse else (col <= row)
    strict = (col > row) if reverse else (col < row)
    return row, col, incl, strict


def _cum_consts(tb):
    idx = jnp.arange(tb)
    same = (idx[:, None] // CHUNK) == (idx[None, :] // CHUNK)
    fwd = same & (idx[None, :] <= idx[:, None])
    bwd = same & (idx[None, :] >= idx[:, None])
    cat = lambda tri: jnp.concatenate([tri, same], axis=0).astype(BF16)
    return cat(fwd), cat(bwd)


def _mm01(m01, x):
    hi = x.astype(BF16)
    r1 = x - hi.astype(F32)
    mid = r1.astype(BF16)
    lo = (r1 - mid.astype(F32)).astype(BF16)
    dot = lambda p: jnp.dot(m01, p, preferred_element_type=F32)
    return dot(hi) + dot(mid) + dot(lo)


def _gla_kernel(qf_ref, kf_ref, vf_ref, gf_ref, qb_ref, kb_ref, vb_ref, gb_ref,
                upf_ref, bf_ref, upb_ref, bb_ref, cf_ref, cb_ref, of_ref, ob_ref,
                s_ref, qt_s, kt_s, qs_s, ks_s, v_s, dec_s, *, tb):
    @pl.when(pl.program_id(1) == 0)
    def _():
        s_ref[...] = jnp.zeros_like(s_ref)

    streams = ((qf_ref, kf_ref, vf_ref, gf_ref, upf_ref, bf_ref, cf_ref),
               (qb_ref, kb_ref, vb_ref, gb_ref, upb_ref, bb_ref, cb_ref))
    for d, (q_ref, k_ref, v_ref, g_ref, up_ref, gbias_ref, cat_ref) in enumerate(streams):
        z = _mm(g_ref[...], up_ref[...]) + gbias_ref[...]
        log_a = -_softplus(-z) / GLA_TAU
        ct = _mm01(cat_ref[...], log_a)
        b = ct[:tb]
        tot = ct[tb:]
        q = q_ref[...] * (GLA_DK ** -0.5)
        k = k_ref[...]
        qt_s[d] = (q * jnp.exp(b - 0.5 * tot)).astype(BF16)
        kt_s[d] = (k * jnp.exp(0.5 * tot - b)).astype(BF16)
        qs_s[d] = (q * jnp.exp(b)).astype(BF16)
        ks_s[d] = (k * jnp.exp(tot - b)).astype(BF16)
        dec_s[d] = jnp.exp(tot)
        v_s[d] = v_ref[...].astype(BF16)

    incl = (_tri_masks(False)[2], _tri_masks(True)[2])
    outs = (of_ref, ob_ref)
    n_chunk = tb // CHUNK

    def body(ci, carry):
        work = []
        for d in range(2):
            c = ci if d == 0 else n_chunk - 1 - ci
            start = pl.multiple_of(c * CHUNK, CHUNK)
            sl = pl.ds(start, CHUNK)
            qt, kt, qs, ks, v = (s[d, sl, :] for s in (qt_s, kt_s, qs_s, ks_s, v_s))
            dec = dec_s[d, pl.ds(start, 1), :]
            for h in range(GLA_HEADS):
                hs = slice(h * GLA_DK, (h + 1) * GLA_DK)
                vs = slice(h * GLA_DV, (h + 1) * GLA_DV)
                work.append((d, h, sl, vs, qt[:, hs], kt[:, hs], qs[:, hs], ks[:, hs], v[:, vs], dec[:, hs],
                             s_ref[d, h]))
        scores = [jnp.where(incl[w[0]], _mm(w[4], w[5], _NT), 0.0) for w in work]
        inter = [_mm(w[6], w[10], _NT) for w in work]
        kv = [_mm(w[8], w[7], _TN) for w in work]
        intra = [_mm(sc, w[8], _NN) for sc, w in zip(scores, work)]
        for w, o_inter, o_intra, kv_n in zip(work, inter, intra, kv):
            d, h, sl, vs = w[:4]
            outs[d][sl, vs] = o_intra + o_inter
            s_ref[d, h] = w[10] * w[9] + kv_n
        return carry

    lax.fori_loop(0, n_chunk, body, 0)


def _gla(pa, up_f, bias_f, up_b, bias_b, *, tb=256):
    bsz, t_len, _ = pa.shape
    nt = t_len // tb
    cat_f, cat_b = _cum_consts(tb)

    def stream(tmap):
        return [pl.BlockSpec((None, tb, GLA_QK), lambda b, t: (b, tmap(t), 0)),
                pl.BlockSpec((None, tb, GLA_QK), lambda b, t: (b, tmap(t), 1)),
                pl.BlockSpec((None, tb, GLA_W), lambda b, t: (b, tmap(t), 1)),
                pl.BlockSpec((None, tb, 128), lambda b, t: (b, tmap(t), 12))]

    fwd = lambda t: t
    bwd = lambda t: nt - 1 - t
    out_shape = jax.ShapeDtypeStruct((bsz, t_len, GLA_W), F32)
    return pl.pallas_call(
        functools.partial(_gla_kernel, tb=tb),
        grid=(bsz, nt),
        in_specs=stream(fwd) + stream(bwd)
                 + [_const_spec((128, GLA_QK)), _const_spec((1, GLA_QK)),
                    _const_spec((128, GLA_QK)), _const_spec((1, GLA_QK)),
                    _const_spec((2 * tb, tb)), _const_spec((2 * tb, tb))],
        out_specs=[pl.BlockSpec((None, tb, GLA_W), lambda b, t: (b, fwd(t), 0)),
                   pl.BlockSpec((None, tb, GLA_W), lambda b, t: (b, bwd(t), 0))],
        out_shape=[out_shape, out_shape],
        scratch_shapes=[pltpu.VMEM((2, GLA_HEADS, GLA_DV, GLA_DK), F32)]
                       + [pltpu.VMEM((2, tb, GLA_QK), BF16)] * 4
                       + [pltpu.VMEM((2, tb, GLA_W), BF16), pltpu.VMEM((2, tb, GLA_QK), F32)],
        compiler_params=_cp(2),
        name="gla",
    )(pa, pa, pa, pa, pa, pa, pa, pa, up_f, bias_f, up_b, bias_b, cat_f, cat_b)


def _seg_sum(x, seg):
    hi = x.astype(BF16)
    lo = (x - hi.astype(F32)).astype(BF16)
    return (jnp.dot(hi, seg, preferred_element_type=F32) + jnp.dot(lo, seg, preferred_element_type=F32))


def _prep_kernel(p_ref, hp_ref, hn_ref, mup_ref, mun_ref, w0f_ref, upf_ref, w0b_ref, upb_ref,
                 a0_ref, aup_ref, gup_ref, kk_ref, ka_ref, rk_ref, seg_ref, cf_ref, cb_ref,
                 v_o, gate_o, bonus_o, knf_o, rf_o, bf_o, kf_o, etf_o, knb_o, rb_o, bb_o, kb_o, etb_o,
                 *, tb, nt):
    t = pl.program_id(1)
    p = p_ref[...]
    row = lax.broadcasted_iota(jnp.int32, (tb, 1), 0)
    halo_prev = jnp.where(t > 0, hp_ref[7:8, :], 0.0)
    halo_next = jnp.where(t < nt - 1, hn_ref[0:1, :], 0.0)
    prev = jnp.where(row == 0, halo_prev, pltpu.roll(p, 1, 0))
    nxt = jnp.where(row == tb - 1, halo_next, pltpu.roll(p, tb - 1, 0))
    rw = p + mup_ref[...] * (prev - p) + mun_ref[...] * (nxt - p)

    r = rw[:, 0:512]
    kr = rw[:, 512:1024]
    vr = rw[:, 1024:1536]
    wd = jnp.tanh(rw[:, 1536:1664])
    ad = rw[:, 1664:1792]
    gd = jax.nn.sigmoid(rw[:, 1792:1920])

    a = jax.nn.sigmoid(a0_ref[...] + _mm(ad, aup_ref[...]))
    gate_o[...] = _mm(gd, gup_ref[...])
    seg = seg_ref[...]
    kk = kr * kk_ref[...]
    kn = kk / jnp.maximum(jnp.sqrt(_seg_sum(kk * kk, seg)), 1e-12)
    k2 = kr * (1.0 + (a - 1.0) * ka_ref[...])
    bonus_o[...] = _seg_sum(r * k2 * rk_ref[...], seg) * vr
    v_o[...] = vr.astype(BF16)
    b = a * kn

    dirs = ((w0f_ref, upf_ref, cf_ref, knf_o, rf_o, bf_o, kf_o, etf_o),
            (w0b_ref, upb_ref, cb_ref, knb_o, rb_o, bb_o, kb_o, etb_o))
    for w0_ref, up_ref, cat_ref, kn_o, r_o, b_o, k_o, et_o in dirs:
        w = -_softplus(-(w0_ref[...] + _mm(wd, up_ref[...]))) - 0.5
        lw = -jnp.exp(w)
        ct = _mm01(cat_ref[...], lw)
        cs = ct[:tb]
        e_neg = jnp.exp(-cs)
        kn_o[...] = (kn * jnp.exp(cs - lw)).astype(BF16)
        r_o[...] = (r * jnp.exp(cs)).astype(BF16)
        b_o[...] = (b * e_neg).astype(BF16)
        k_o[...] = (k2 * e_neg).astype(BF16)
        e_tot = jnp.exp(ct[tb:])
        for c in range(tb // CHUNK):
            et_o[c] = e_tot[c * CHUNK:c * CHUNK + 1, :]


def _rwkv_prep(pb, prm, *, tb=256):
    bsz, t_len, _ = pb.shape
    nt = t_len // tb
    hb = tb // 8
    n8 = t_len // 8
    n_chunk = tb // CHUNK
    cat_f, cat_b = _cum_consts(tb)
    consts = [prm["mu_prev"], prm["mu_next"], prm["w0_f"], prm["up_f"], prm["w0_b"], prm["up_b"],
              prm["a0"], prm["a_up"], prm["g_up"], prm["k_k"], prm["k_a"], prm["r_k"], prm["seg"],
              cat_f, cat_b]
    row_spec = pl.BlockSpec((None, tb, RW_W), lambda b, t: (b, t, 0))
    rows = lambda dt: jax.ShapeDtypeStruct((bsz, t_len, RW_W), dt)
    et_spec = pl.BlockSpec((None, n_chunk, 1, RW_W), lambda b, t: (b, t, 0, 0))
    et = jax.ShapeDtypeStruct((bsz, t_len // CHUNK, 1, RW_W), F32)
    per_dir_specs = [row_spec] * 4 + [et_spec]
    per_dir_shapes = [rows(BF16)] * 4 + [et]
    return pl.pallas_call(
        functools.partial(_prep_kernel, tb=tb, nt=nt),
        grid=(bsz, nt),
        in_specs=[pl.BlockSpec((None, tb, PB_COLS), lambda b, t: (b, t, 0)),
                  pl.BlockSpec((None, 8, PB_COLS), lambda b, t: (b, jnp.maximum(t * hb - 1, 0), 0)),
                  pl.BlockSpec((None, 8, PB_COLS), lambda b, t: (b, jnp.minimum((t + 1) * hb, n8 - 1), 0))]
                 + [_const_spec(c.shape) for c in consts],
        out_specs=[row_spec] * 3 + per_dir_specs * 2,
        out_shape=[rows(BF16), rows(F32), rows(F32)] + per_dir_shapes * 2,
        compiler_params=_cp(2),
        name="rwkv_prep",
    )(pb, pb, pb, *consts)


def _pair_masks(reverse):
    t = lax.broadcasted_iota(jnp.int32, (CHUNK, PAIR), 0)
    s = lax.broadcasted_iota(jnp.int32, (CHUNK, PAIR), 1) % RW_N
    incl = (s >= t) if reverse else (s <= t)
    strict = (s > t) if reverse else (s < t)
    levels = []
    m = 1
    while m < CHUNK:
        if reverse:
            levels.append(((t // m) % 2 == 0) & (s // m == t // m + 1))
        else:
            levels.append(((t // m) % 2 == 1) & (s // m == t // m - 1))
        m *= 2
    return incl, strict, levels


def _dir_where(mask_f, mask_b, x):
    h = x.shape[0] // 2
    return jnp.concatenate([jnp.where(mask_f[None], x[:h], 0.0), jnp.where(mask_b[None], x[h:], 0.0)], axis=0)


def _block_diag(x):
    rows = lax.broadcasted_iota(jnp.int32, (2 * CHUNK, PAIR), 0) // CHUNK
    lanes = lax.broadcasted_iota(jnp.int32, (2 * CHUNK, PAIR), 1) // RW_N
    x = x.astype(BF16)
    return jnp.where((rows == lanes)[None], jnp.concatenate([x, x], axis=1), jnp.zeros((), BF16))


def _chunk_operators(kn, r, b, k, v, e_tot):
    incl_f, strict_f, levels_f = _pair_masks(False)
    incl_b, strict_b, levels_b = _pair_masks(True)
    mask2_f = jnp.concatenate([strict_f, incl_f], axis=0)
    mask2_b = jnp.concatenate([strict_b, incl_b], axis=0)
    t = lax.broadcasted_iota(jnp.int32, (CHUNK, PAIR), 0)
    s = lax.broadcasted_iota(jnp.int32, (CHUNK, PAIR), 1) % RW_N
    eye = (t == s).astype(F32)
    same_head = (lax.broadcasted_iota(jnp.int32, (PAIR, PAIR), 0) // RW_N
                 == lax.broadcasted_iota(jnp.int32, (PAIR, PAIR), 1) // RW_N)[None]
    xr = jnp.concatenate([kn, r], axis=1)
    ab = _dir_where(mask2_f, mask2_b, _bmm(xr, _block_diag(b), 2, 2))
    ak = _dir_where(mask2_f, mask2_b, _bmm(xr, _block_diag(k), 2, 2))
    av = _bmm(ak, _block_diag(v), 2, 1)
    a_b = ab[:, :CHUNK]
    a_rb = ab[:, CHUNK:]
    inv = eye[None] - _dir_where(levels_f[0], levels_b[0], a_b)
    for mk_f, mk_b in zip(levels_f[1:], levels_b[1:]):
        x = _bmm(_dir_where(mk_f, mk_b, a_b), _block_diag(inv), 2, 1)
        inv = inv - _bmm(inv, _block_diag(x), 2, 1)
    gw = _bmm(inv, jnp.concatenate([_block_diag(kn), _block_diag(av[:, :CHUNK])], axis=2), 2, 1)
    g = gw[:, :, :PAIR]
    w = gw[:, :, PAIR:]
    corr = _bmm(a_rb, jnp.concatenate([_block_diag(g), _block_diag(w)], axis=2), 2, 1)
    r_op = r.astype(F32) - corr[:, :, :PAIR]
    y0 = av[:, CHUNK:] - corr[:, :, PAIR:]
    p_op = jnp.where(same_head, _bmm(g, b, 1, 1), 0.0) * (-e_tot)
    vw = jnp.concatenate([v, w.astype(BF16)], axis=1)
    kb = jnp.concatenate([k, -b], axis=1)
    q_op = jnp.where(same_head, _bmm(vw, kb, 1, 1), 0.0) * e_tot
    return r_op, y0, p_op, q_op


def _scan_kernel(knf_ref, rf_ref, bf_ref, kf_ref, vf_ref, etf_ref,
                 knb_ref, rb_ref, bb_ref, kb_ref, vb_ref, etb_ref, yf_ref, yb_ref, s_ref, *, tb):
    @pl.when(pl.program_id(1) == 0)
    def _():
        s_ref[...] = jnp.zeros_like(s_ref)

    n_chunk = tb // CHUNK
    streams = ((knf_ref, rf_ref, bf_ref, kf_ref, vf_ref), (knb_ref, rb_ref, bb_ref, kb_ref, vb_ref))

    def stacked(ref_f, ref_b, rows):
        tiles = []
        for ref in (ref_f, ref_b):
            for c in range(n_chunk):
                x = ref[c * rows:(c + 1) * rows, :] if rows == CHUNK else ref[c]
                tiles += [x[:, p * PAIR:(p + 1) * PAIR] for p in range(N_PAIR)]
        return jnp.stack(tiles)

    ops = [stacked(streams[0][i], streams[1][i], CHUNK) for i in range(5)]
    e_tot = stacked(etf_ref, etb_ref, 1)
    r_op, y0, p_op, q_op = _chunk_operators(*ops, e_tot)

    state = s_ref[...]
    half = n_chunk * N_PAIR
    for i in range(n_chunk):
        cf, cb = i, n_chunk - 1 - i
        sel = lambda x: jnp.concatenate([x[cf * N_PAIR:(cf + 1) * N_PAIR],
                                         x[half + cb * N_PAIR:half + (cb + 1) * N_PAIR]], axis=0)
        y = _bmm(sel(r_op), state, 2, 2) + sel(y0)
        state = state * sel(e_tot) + _bmm(state, sel(p_op), 2, 1) + sel(q_op)
        for p in range(N_PAIR):
            yf_ref[cf * CHUNK:(cf + 1) * CHUNK, p * PAIR:(p + 1) * PAIR] = y[p]
            yb_ref[cb * CHUNK:(cb + 1) * CHUNK, p * PAIR:(p + 1) * PAIR] = y[N_PAIR + p]
    s_ref[...] = state


def _rwkv_scan(v, fwd_ops, bwd_ops, *, tb=256):
    bsz, t_len, _ = v.shape
    nt = t_len // tb
    n_chunk = tb // CHUNK

    def stream(tmap):
        rows = pl.BlockSpec((None, tb, RW_W), lambda bi, ti: (bi, tmap(ti), 0))
        et = pl.BlockSpec((None, n_chunk, 1, RW_W), lambda bi, ti: (bi, tmap(ti), 0, 0))
        return rows, [rows] * 5 + [et]

    rows_f, specs_f = stream(lambda t: t)
    rows_b, specs_b = stream(lambda t: nt - 1 - t)
    out_shape = jax.ShapeDtypeStruct(v.shape, F32)
    kn_f, r_f, b_f, k_f, et_f = fwd_ops
    kn_b, r_b, b_b, k_b, et_b = bwd_ops
    return pl.pallas_call(
        functools.partial(_scan_kernel, tb=tb),
        grid=(bsz, nt),
        in_specs=specs_f + specs_b,
        out_specs=[rows_f, rows_b],
        out_shape=[out_shape, out_shape],
        scratch_shapes=[pltpu.VMEM((2 * N_PAIR, PAIR, PAIR), F32)],
        compiler_params=_cp(2),
        name="rwkv_scan",
    )(kn_f, r_f, b_f, k_f, v, et_f, kn_b, r_b, b_b, k_b, v, et_b)


def _mixout_kernel(of_ref, ob_ref, g_ref, gn_ref, yf_ref, yb_ref, gate_ref, bonus_ref, lg_ref, lb_ref,
                   seg_ref, x_ref, w_ref, ng_ref, nb_ref, o_ref):
    o = of_ref[...] + ob_ref[...]
    parts = []
    for h in range(GLA_HEADS):
        oh = o[:, h * GLA_DV:(h + 1) * GLA_DV]
        parts.append(oh * lax.rsqrt(jnp.mean(oh * oh, axis=-1, keepdims=True) + LN_EPS))
    g = g_ref[...]
    o = jnp.concatenate(parts, axis=-1) * gn_ref[...] * (g * jax.nn.sigmoid(g))
    y = yf_ref[...] + yb_ref[...]
    seg = seg_ref[...]
    yc = y - _seg_sum(y, seg) * (1.0 / RW_N)
    var = _seg_sum(yc * yc, seg) * (1.0 / RW_N)
    y = yc * lax.rsqrt(var + RW_GN_EPS) * lg_ref[...] + lb_ref[...]
    y = (y + bonus_ref[...]) * gate_ref[...]
    mixed = jnp.concatenate([o, y], axis=-1).astype(BF16)
    tm_out = jnp.dot(mixed, w_ref[...], preferred_element_type=F32)
    o_ref[...] = _layer_norm(ALPHA * x_ref[...] + tm_out, ng_ref[...], nb_ref[...], LN_EPS)


def _mix_out(o_f, o_b, pa, gla_norm_g, y_f, y_b, gate, bonus, lnx_g, lnx_b, seg, x, w_out, ln_g, ln_b,
             *, tm=256):
    bsz, t_len, _ = x.shape
    row = lambda width: pl.BlockSpec((None, tm, width), lambda b, t: (b, t, 0))
    return pl.pallas_call(
        _mixout_kernel,
        grid=(bsz, t_len // tm),
        in_specs=[row(GLA_W), row(GLA_W),
                  pl.BlockSpec((None, tm, GLA_W), lambda b, t: (b, t, 2)),
                  _const_spec((1, GLA_W)),
                  row(RW_W), row(RW_W), row(RW_W), row(RW_W),
                  _const_spec((1, RW_W)), _const_spec((1, RW_W)), _const_spec((RW_W, RW_W)),
                  row(D_MODEL),
                  _const_spec((D_MODEL, D_MODEL)),
                  _const_spec((1, D_MODEL)), _const_spec((1, D_MODEL))],
        out_specs=row(D_MODEL),
        out_shape=jax.ShapeDtypeStruct(x.shape, F32),
        compiler_params=_cp(2),
        name="mix_out",
    )(o_f, o_b, pa, gla_norm_g, y_f, y_b, gate, bonus, lnx_g, lnx_b, seg, x, w_out, ln_g, ln_b)


def _memkv_kernel(m_ref, g_ref, b_ref, w_ref, k_ref, v_ref):
    m = _layer_norm(m_ref[...], g_ref[...], b_ref[...], LN_EPS)
    kv = jnp.dot(m.astype(BF16), w_ref[...], preferred_element_type=F32)
    k_ref[...] = kv[:, :D_MODEL]
    v_ref[...] = kv[:, D_MODEL:]


def _mem_kv(mem, g, b, w_kv):
    bsz, m_tok, _ = mem.shape
    spec = pl.BlockSpec((None, m_tok, D_MODEL), lambda i: (i, 0, 0))
    shape = jax.ShapeDtypeStruct(mem.shape, F32)
    return pl.pallas_call(
        _memkv_kernel,
        grid=(bsz,),
        in_specs=[spec, _const_spec((1, D_MODEL)), _const_spec((1, D_MODEL)),
                  _const_spec((D_MODEL, 2 * D_MODEL))],
        out_specs=[spec, spec],
        out_shape=[shape, shape],
        compiler_params=_cp(1),
        name="mem_kv",
    )(mem, g, b, w_kv)


def _ca_kernel(x_ref, k_ref, v_ref, wq_ref, wo_ref, g_ref, b_ref, o_ref):
    x = x_ref[...]
    q = jnp.dot(x.astype(BF16), wq_ref[...], preferred_element_type=F32)
    k = k_ref[...]
    v = v_ref[...]
    parts = []
    for h in range(MEM_HEADS):
        hs = slice(h * MEM_HD, (h + 1) * MEM_HD)
        s = _mm(q[:, hs], k[:, hs], _NT) * (MEM_HD ** -0.5)
        e = jnp.exp(s - jnp.max(s, axis=-1, keepdims=True))
        p = e / jnp.sum(e, axis=-1, keepdims=True)
        parts.append(_mm(p, v[:, hs], _NN))
    ca = jnp.dot(jnp.concatenate(parts, axis=-1).astype(BF16), wo_ref[...], preferred_element_type=F32)
    o_ref[...] = _layer_norm(ALPHA * x + ca, g_ref[...], b_ref[...], LN_EPS)


def _cross_attn(x, k, v, w_q, w_o, g, b, *, tm=512):
    bsz, t_len, _ = x.shape
    m_tok = k.shape[1]
    row = pl.BlockSpec((None, tm, D_MODEL), lambda bi, t: (bi, t, 0))
    mem = pl.BlockSpec((None, m_tok, D_MODEL), lambda bi, t: (bi, 0, 0))
    return pl.pallas_call(
        _ca_kernel,
        grid=(bsz, t_len // tm),
        in_specs=[row, mem, mem, _const_spec((D_MODEL, D_MODEL)), _const_spec((D_MODEL, D_MODEL)),
                  _const_spec((1, D_MODEL)), _const_spec((1, D_MODEL))],
        out_specs=row,
        out_shape=jax.ShapeDtypeStruct(x.shape, F32),
        compiler_params=_cp(2),
        name="cross_attn",
    )(x, k, v, w_q, w_o, g, b)


def _pad_rows(w, start, total):
    return jnp.zeros((total, w.shape[1]), w.dtype).at[start:start + w.shape[0]].set(w)


def _layer(x, mem, p):
    bsz, t_len, _ = x.shape
    x2d = x.reshape(bsz * t_len, D_MODEL)
    x1 = _ffn_block(x2d, p["ffn1_w_in"], p["ffn1_w_out"], p["ln_ffn1_g"], p["ln_ffn1_b"])
    pa, pb = _in_proj(x1, p["w_pa"], p["w_pb"])
    pa = pa.reshape(bsz, t_len, PA_COLS)
    pb = pb.reshape(bsz, t_len, PB_COLS)
    o_f, o_b = _gla(pa, p["gla_up_f"], p["gla_b_f"], p["gla_up_b"], p["gla_b_b"])
    v, gate, bonus, *ops = _rwkv_prep(pb, p["rw"])
    y_f, y_b = _rwkv_scan(v, ops[:5], ops[5:])
    x2 = _mix_out(o_f, o_b, pa, p["gla_norm_g"], y_f, y_b, gate, bonus, p["lnx_g"], p["lnx_b"], p["rw"]["seg"],
                  x1.reshape(bsz, t_len, D_MODEL), p["w_out"], p["ln_mix_g"], p["ln_mix_b"])
    mk, mv = _mem_kv(mem, p["mem_ln_g"], p["mem_ln_b"], p["ca_w_kv"])
    x3 = _cross_attn(x2, mk, mv, p["ca_w_q"], p["ca_w_o"], p["ln_ca_g"], p["ln_ca_b"])
    x4 = _ffn_block(x3.reshape(bsz * t_len, D_MODEL), p["ffn2_w_in"], p["ffn2_w_out"],
                    p["ln_ffn2_g"], p["ln_ffn2_b"])
    return x4.reshape(bsz, t_len, D_MODEL)


def kernel(x_prompt, x_sample, mem_prompt, mem_sample, ffn1_w_in, ffn1_w_out, ln_ffn1_g, ln_ffn1_b, mix_w_in, gla_gate_up_fwd, gla_gate_b_fwd, gla_gate_up_bwd, gla_gate_b_bwd, gla_norm_g, rwkv_mu_prev, rwkv_mu_next, rwkv_w0_fwd, rwkv_w_up_fwd, rwkv_w0_bwd, rwkv_w_up_bwd, rwkv_a0, rwkv_a_up, rwkv_g_up, rwkv_k_k, rwkv_k_a, rwkv_r_k, rwkv_lnx_g, rwkv_lnx_b, mix_w_out, ln_mix_g, ln_mix_b, mem_ln_g, mem_ln_b, ca_w_q, ca_w_kv, ca_w_o, ln_ca_g, ln_ca_b, ffn2_w_in, ffn2_w_out, ln_ffn2_g, ln_ffn2_b):
    y_prompt, y_sample = x_prompt, x_sample
    for l in range(DEPTH):
        row = lambda a: a[l].reshape(1, -1)
        w_in = mix_w_in[l]
        zeros = lambda n: jnp.zeros((D_MODEL, n), F32)
        rw_off = GLA_COLS
        lr_off = rw_off + 3 * RW_W
        gd_off = lr_off + 2 * RW_DECAY_RANK + RW_AAA_RANK
        w_pa = jnp.concatenate([w_in[:, :GLA_COLS], zeros(PA_COLS - GLA_COLS)], axis=1)
        w_pb = jnp.concatenate([w_in[:, rw_off:gd_off], zeros(64), w_in[:, gd_off:]], axis=1)
        perm_mu = lambda mu: jnp.concatenate(
            [mu[:gd_off - rw_off], jnp.zeros((64,), F32), mu[gd_off - rw_off:]]).reshape(1, PB_COLS)
        rw = {
            "mu_prev": perm_mu(rwkv_mu_prev[l]), "mu_next": perm_mu(rwkv_mu_next[l]),
            "w0_f": row(rwkv_w0_fwd), "up_f": _pad_rows(rwkv_w_up_fwd[l], 0, 128).astype(BF16),
            "w0_b": row(rwkv_w0_bwd), "up_b": _pad_rows(rwkv_w_up_bwd[l], RW_DECAY_RANK, 128).astype(BF16),
            "a0": row(rwkv_a0), "a_up": _pad_rows(rwkv_a_up[l], 0, 128).astype(BF16),
            "g_up": rwkv_g_up[l].astype(BF16),
            "k_k": row(rwkv_k_k), "k_a": row(rwkv_k_a), "r_k": rwkv_r_k[l].reshape(1, RW_W),
            "seg": jnp.kron(jnp.eye(RW_HEADS, dtype=F32), jnp.ones((RW_N, RW_N), F32)).astype(BF16),
        }
        p = {
            "ffn1_w_in": ffn1_w_in[l].astype(BF16), "ffn1_w_out": ffn1_w_out[l].astype(BF16),
            "ln_ffn1_g": row(ln_ffn1_g), "ln_ffn1_b": row(ln_ffn1_b),
            "w_pa": w_pa.astype(BF16), "w_pb": w_pb.astype(BF16),
            "gla_up_f": _pad_rows(gla_gate_up_fwd[l], 0, 128), "gla_b_f": row(gla_gate_b_fwd),
            "gla_up_b": _pad_rows(gla_gate_up_bwd[l], GLA_RANK, 128), "gla_b_b": row(gla_gate_b_bwd),
            "gla_norm_g": row(gla_norm_g),
            "rw": rw,
            "lnx_g": row(rwkv_lnx_g), "lnx_b": row(rwkv_lnx_b),
            "w_out": mix_w_out[l].astype(BF16),
            "ln_mix_g": row(ln_mix_g), "ln_mix_b": row(ln_mix_b),
            "mem_ln_g": row(mem_ln_g), "mem_ln_b": row(mem_ln_b),
            "ca_w_q": ca_w_q[l].astype(BF16), "ca_w_kv": ca_w_kv[l].astype(BF16), "ca_w_o": ca_w_o[l].astype(BF16),
            "ln_ca_g": row(ln_ca_g), "ln_ca_b": row(ln_ca_b),
            "ffn2_w_in": ffn2_w_in[l].astype(BF16), "ffn2_w_out": ffn2_w_out[l].astype(BF16),
            "ln_ffn2_g": row(ln_ffn2_g), "ln_ffn2_b": row(ln_ffn2_b),
        }
        y_prompt = _layer(y_prompt, mem_prompt, p)
        y_sample = _layer(y_sample, mem_sample, p)
    return (y_prompt, y_sample)
```

```python
import functools

import jax
import jax.numpy as jnp
from jax import lax
from jax.experimental import pallas as pl
from jax.experimental.pallas import tpu as pltpu

F32 = jnp.float32
BF16 = jnp.bfloat16

D_MODEL = 1024
D_FF = 2816
DEPTH = 1
GLA_HEADS = 4
GLA_DK = 64
GLA_DV = 128
GLA_QK = GLA_HEADS * GLA_DK
GLA_W = GLA_HEADS * GLA_DV
GLA_RANK = 16
GLA_TAU = 16.0
RW_HEADS = 8
RW_N = 64
RW_W = RW_HEADS * RW_N
RW_DECAY_RANK = 64
RW_AAA_RANK = 64
RW_GATE_RANK = 128
RW_GN_EPS = 64e-5
GLA_COLS = 1568
RW_COLS = 1856
MEM_HEADS = 4
MEM_HD = D_MODEL // MEM_HEADS
LN_EPS = 1e-5
ALPHA = (2.0 * DEPTH) ** 0.25
CHUNK = 64
PAIR = 2 * RW_N
N_PAIR = RW_HEADS // 2
DECAY_SCALE = 0.6065306597126334

PA_COLS = 1664
PB_COLS = 1920

VMEM_LIMIT = 56 * 1024 * 1024


def _cp(n_axes):
    return pltpu.CompilerParams(dimension_semantics=("arbitrary",) * n_axes,
                                vmem_limit_bytes=VMEM_LIMIT)


def _mm(a, b, dims=((1,), (0,))):
    return lax.dot_general(a.astype(BF16), b.astype(BF16), (dims, ((), ())), preferred_element_type=F32)


_NN = ((1,), (0,))
_NT = ((1,), (1,))
_TN = ((0,), (0,))


def _bmm(a, b, ca, cb):
    return lax.dot_general(a.astype(BF16), b.astype(BF16), (((ca,), (cb,)), ((0,), (0,))),
                           preferred_element_type=F32)


def _layer_norm(z, g, b, eps):
    mu = jnp.mean(z, axis=-1, keepdims=True)
    zc = z - mu
    var = jnp.mean(zc * zc, axis=-1, keepdims=True)
    return zc * lax.rsqrt(var + eps) * g + b


def _softplus(x):
    return jnp.maximum(x, 0.0) + jnp.log(1.0 + jnp.exp(-jnp.abs(x)))


def _const_spec(shape):
    nd = len(shape)
    return pl.BlockSpec(shape, lambda *_: (0,) * nd, pipeline_mode=pl.Buffered(1))


def _ffn_kernel(x_ref, win_ref, wout_ref, g_ref, b_ref, o_ref, *, n_split):
    x = x_ref[...]
    xb = x.astype(BF16)
    fc = D_FF // n_split
    acc = None
    for f in range(n_split):
        gate = jnp.dot(xb, win_ref[:, f * fc:(f + 1) * fc], preferred_element_type=F32)
        up = jnp.dot(xb, win_ref[:, D_FF + f * fc:D_FF + (f + 1) * fc], preferred_element_type=F32)
        h = (gate * jax.nn.sigmoid(gate) * up).astype(BF16)
        y = jnp.dot(h, wout_ref[f * fc:(f + 1) * fc, :], preferred_element_type=F32)
        acc = y if acc is None else acc + y
    o_ref[...] = _layer_norm(ALPHA * x + 0.5 * acc, g_ref[...], b_ref[...], LN_EPS)


def _ffn_block(x, w_in, w_out, g, b, *, tm=512, n_split=2):
    n = x.shape[0]
    return pl.pallas_call(
        functools.partial(_ffn_kernel, n_split=n_split),
        grid=(n // tm,),
        in_specs=[pl.BlockSpec((tm, D_MODEL), lambda i: (i, 0)),
                  _const_spec((D_MODEL, 2 * D_FF)),
                  _const_spec((D_FF, D_MODEL)),
                  _const_spec((1, D_MODEL)),
                  _const_spec((1, D_MODEL))],
        out_specs=pl.BlockSpec((tm, D_MODEL), lambda i: (i, 0)),
        out_shape=jax.ShapeDtypeStruct((n, D_MODEL), F32),
        compiler_params=_cp(1),
        name="ffn_block",
    )(x, w_in, w_out, g, b)


def _inproj_kernel(x_ref, wa_ref, wb_ref, oa_ref, ob_ref):
    xb = x_ref[...].astype(BF16)
    oa_ref[...] = jnp.dot(xb, wa_ref[...], preferred_element_type=F32)
    ob_ref[...] = jnp.dot(xb, wb_ref[...], preferred_element_type=F32)


def _in_proj(x, wa, wb, *, tm=512):
    n = x.shape[0]
    return pl.pallas_call(
        _inproj_kernel,
        grid=(n // tm,),
        in_specs=[pl.BlockSpec((tm, D_MODEL), lambda i: (i, 0)),
                  _const_spec((D_MODEL, PA_COLS)),
                  _const_spec((D_MODEL, PB_COLS))],
        out_specs=[pl.BlockSpec((tm, PA_COLS), lambda i: (i, 0)),
                   pl.BlockSpec((tm, PB_COLS), lambda i: (i, 0))],
        out_shape=[jax.ShapeDtypeStruct((n, PA_COLS), F32),
                   jax.ShapeDtypeStruct((n, PB_COLS), F32)],
        compiler_params=_cp(1),
        name="in_proj",
    )(x, wa, wb)


def _tri_masks(reverse):
    row = lax.broadcasted_iota(jnp.int32, (CHUNK, CHUNK), 0)
    col = lax.broadcasted_iota(jnp.int32, (CHUNK, CHUNK), 1)
    incl = (col >= row) if reverse else (col <= row)
    strict = (col > row) if reverse else (col < row)
    return row, col, incl, strict


def _cum_consts(tb):
    idx = jnp.arange(tb)
    same = (idx[:, None] // CHUNK) == (idx[None, :] // CHUNK)
    fwd = same & (idx[None, :] <= idx[:, None])
    bwd = same & (idx[None, :] >= idx[:, None])
    cat = lambda tri: jnp.concatenate([tri, same], axis=0).astype(BF16)
    return cat(fwd), cat(bwd)


def _mm01(m01, x):
    hi = x.astype(BF16)
    r1 = x - hi.astype(F32)
    mid = r1.astype(BF16)
    lo = (r1 - mid.astype(F32)).astype(BF16)
    dot = lambda p: jnp.dot(m01, p, preferred_element_type=F32)
    return dot(hi) + dot(mid) + dot(lo)


def _gla_kernel(qf_ref, kf_ref, vf_ref, gf_ref, qb_ref, kb_ref, vb_ref, gb_ref,
                upf_ref, bf_ref, upb_ref, bb_ref, cf_ref, cb_ref, of_ref, ob_ref, s_ref, *, tb):
    @pl.when(pl.program_id(1) == 0)
    def _():
        s_ref[...] = jnp.zeros_like(s_ref)

    n_chunk = tb // CHUNK
    n_pair = GLA_HEADS // 2
    pv = 2 * GLA_DV
    streams = ((qf_ref, kf_ref, vf_ref, gf_ref, upf_ref, bf_ref, cf_ref),
               (qb_ref, kb_ref, vb_ref, gb_ref, upb_ref, bb_ref, cb_ref))
    qt_t, kt_t, qs_t, ks_t, v_t, dec_t = [], [], [], [], [], []
    for q_ref, k_ref, v_ref, g_ref, up_ref, gbias_ref, cat_ref in streams:
        z = _mm(g_ref[...], up_ref[...]) + gbias_ref[...]
        log_a = -_softplus(-z) / GLA_TAU
        ct = _mm01(cat_ref[...], log_a)
        b = ct[:tb]
        half = 0.5 * ct[tb:]
        qt = q_ref[...] * (GLA_DK ** -0.5) * jnp.exp(b - half)
        kt = k_ref[...] * jnp.exp(half - b)
        v = v_ref[...].astype(BF16)
        for c in range(n_chunk):
            rows = slice(c * CHUNK, (c + 1) * CHUNK)
            e_half = jnp.exp(half[c * CHUNK:c * CHUNK + 1, :])
            for p in range(n_pair):
                lanes = slice(p * PAIR, (p + 1) * PAIR)
                qt_t.append(qt[rows, lanes])
                kt_t.append(kt[rows, lanes])
                qs_t.append(qt[rows, lanes] * e_half[:, lanes])
                ks_t.append(kt[rows, lanes] * e_half[:, lanes])
                dec_t.append(e_half[:, lanes] * e_half[:, lanes])
                v_t.append(v[rows, p * pv:(p + 1) * pv])
    qt_s, kt_s, qs_s, ks_s, v_s = (jnp.stack(x) for x in (qt_t, kt_t, qs_t, ks_t, v_t))

    incl_f = _pair_masks(False)[0]
    incl_b = _pair_masks(True)[0]
    scores = _dir_where(incl_f, incl_b, _bmm(qt_s, _block_diag(kt_s), 2, 2))
    v_rows = lax.broadcasted_iota(jnp.int32, (PAIR, pv), 0) // CHUNK
    v_lanes = lax.broadcasted_iota(jnp.int32, (PAIR, pv), 1) // GLA_DV
    v_bd = jnp.where((v_rows == v_lanes)[None], jnp.concatenate([v_s, v_s], axis=1), jnp.zeros((), BF16))
    intra = _bmm(scores, v_bd, 2, 1)
    same_head = (lax.broadcasted_iota(jnp.int32, (pv, PAIR), 0) // GLA_DV
                 == lax.broadcasted_iota(jnp.int32, (pv, PAIR), 1) // GLA_DK)
    kv = jnp.where(same_head[None], _bmm(v_s, ks_s, 1, 1), 0.0)

    idx = lambda d, c, p: (d * n_chunk + c) * n_pair + p
    s_prev = [None] * (2 * n_chunk * n_pair)
    for d in range(2):
        for p in range(n_pair):
            state = s_ref[d * n_pair + p]
            for i in range(n_chunk):
                c = i if d == 0 else n_chunk - 1 - i
                s_prev[idx(d, c, p)] = state
                state = state * dec_t[idx(d, c, p)] + kv[idx(d, c, p)]
            s_ref[d * n_pair + p] = state
    out = intra + _bmm(qs_s, jnp.stack(s_prev), 2, 2)
    for d, o_ref in enumerate((of_ref, ob_ref)):
        for c in range(n_chunk):
            for p in range(n_pair):
                o_ref[c * CHUNK:(c + 1) * CHUNK, p * pv:(p + 1) * pv] = out[idx(d, c, p)]


def _gla(pa, up_f, bias_f, up_b, bias_b, *, tb=256):
    bsz, t_len, _ = pa.shape
    nt = t_len // tb
    cat_f, cat_b = _cum_consts(tb)

    def stream(tmap):
        return [pl.BlockSpec((None, tb, GLA_QK), lambda b, t: (b, tmap(t), 0)),
                pl.BlockSpec((None, tb, GLA_QK), lambda b, t: (b, tmap(t), 1)),
                pl.BlockSpec((None, tb, GLA_W), lambda b, t: (b, tmap(t), 1)),
                pl.BlockSpec((None, tb, 128), lambda b, t: (b, tmap(t), 12))]

    fwd = lambda t: t
    bwd = lambda t: nt - 1 - t
    out_shape = jax.ShapeDtypeStruct((bsz, t_len, GLA_W), F32)
    return pl.pallas_call(
        functools.partial(_gla_kernel, tb=tb),
        grid=(bsz, nt),
        in_specs=stream(fwd) + stream(bwd)
                 + [_const_spec((128, GLA_QK)), _const_spec((1, GLA_QK)),
                    _const_spec((128, GLA_QK)), _const_spec((1, GLA_QK)),
                    _const_spec((2 * tb, tb)), _const_spec((2 * tb, tb))],
        out_specs=[pl.BlockSpec((None, tb, GLA_W), lambda b, t: (b, fwd(t), 0)),
                   pl.BlockSpec((None, tb, GLA_W), lambda b, t: (b, bwd(t), 0))],
        out_shape=[out_shape, out_shape],
        scratch_shapes=[pltpu.VMEM((GLA_HEADS, 2 * GLA_DV, 2 * GLA_DK), F32)],
        compiler_params=_cp(2),
        name="gla",
    )(pa, pa, pa, pa, pa, pa, pa, pa, up_f, bias_f, up_b, bias_b, cat_f, cat_b)


def _seg_sum(x, seg):
    hi = x.astype(BF16)
    lo = (x - hi.astype(F32)).astype(BF16)
    return (jnp.dot(hi, seg, preferred_element_type=F32) + jnp.dot(lo, seg, preferred_element_type=F32))


def _prep_kernel(p_ref, hp_ref, hn_ref, mup_ref, mun_ref, w0f_ref, upf_ref, w0b_ref, upb_ref,
                 a0_ref, aup_ref, gup_ref, kk_ref, ka_ref, rk_ref, seg_ref, cf_ref, cb_ref,
                 v_o, gate_o, bonus_o, knf_o, rf_o, bf_o, kf_o, etf_o, knb_o, rb_o, bb_o, kb_o, etb_o,
                 *, tb, nt):
    t = pl.program_id(1)
    p = p_ref[...]
    row = lax.broadcasted_iota(jnp.int32, (tb, 1), 0)
    halo_prev = jnp.where(t > 0, hp_ref[7:8, :], 0.0)
    halo_next = jnp.where(t < nt - 1, hn_ref[0:1, :], 0.0)
    prev = jnp.where(row == 0, halo_prev, pltpu.roll(p, 1, 0))
    nxt = jnp.where(row == tb - 1, halo_next, pltpu.roll(p, tb - 1, 0))
    rw = p + mup_ref[...] * (prev - p) + mun_ref[...] * (nxt - p)

    r = rw[:, 0:512]
    kr = rw[:, 512:1024]
    vr = rw[:, 1024:1536]
    wd = jnp.tanh(rw[:, 1536:1664])
    ad = rw[:, 1664:1792]
    gd = jax.nn.sigmoid(rw[:, 1792:1920])

    a = jax.nn.sigmoid(a0_ref[...] + _mm(ad, aup_ref[...]))
    gate_o[...] = _mm(gd, gup_ref[...])
    seg = seg_ref[...]
    kk = kr * kk_ref[...]
    kn = kk * jnp.minimum(lax.rsqrt(_seg_sum(kk * kk, seg)), 1e12)
    k2 = kr * (1.0 + (a - 1.0) * ka_ref[...])
    bonus_o[...] = _seg_sum(r * k2 * rk_ref[...], seg) * vr
    v_o[...] = vr.astype(BF16)
    b = a * kn

    dirs = ((w0f_ref, upf_ref, cf_ref, knf_o, rf_o, bf_o, kf_o, etf_o),
            (w0b_ref, upb_ref, cb_ref, knb_o, rb_o, bb_o, kb_o, etb_o))
    for w0_ref, up_ref, cat_ref, kn_o, r_o, b_o, k_o, et_o in dirs:
        lw = -DECAY_SCALE * jax.nn.sigmoid(w0_ref[...] + _mm(wd, up_ref[...]))
        ct = _mm01(cat_ref[...], lw)
        cs = ct[:tb]
        e_neg = jnp.exp(-cs)
        kn_o[...] = (kn * jnp.exp(cs - lw)).astype(BF16)
        r_o[...] = (r * jnp.exp(cs)).astype(BF16)
        b_o[...] = (b * e_neg).astype(BF16)
        k_o[...] = (k2 * e_neg).astype(BF16)
        e_tot = jnp.exp(ct[tb:])
        for c in range(tb // CHUNK):
            et_o[c] = e_tot[c * CHUNK:c * CHUNK + 1, :]


def _rwkv_prep(pb, prm, *, tb=256):
    bsz, t_len, _ = pb.shape
    nt = t_len // tb
    hb = tb // 8
    n8 = t_len // 8
    n_chunk = tb // CHUNK
    cat_f, cat_b = _cum_consts(tb)
    consts = [prm["mu_prev"], prm["mu_next"], prm["w0_f"], prm["up_f"], prm["w0_b"], prm["up_b"],
              prm["a0"], prm["a_up"], prm["g_up"], prm["k_k"], prm["k_a"], prm["r_k"], prm["seg"],
              cat_f, cat_b]
    row_spec = pl.BlockSpec((None, tb, RW_W), lambda b, t: (b, t, 0))
    rows = lambda dt: jax.ShapeDtypeStruct((bsz, t_len, RW_W), dt)
    et_spec = pl.BlockSpec((None, n_chunk, 1, RW_W), lambda b, t: (b, t, 0, 0))
    et = jax.ShapeDtypeStruct((bsz, t_len // CHUNK, 1, RW_W), F32)
    per_dir_specs = [row_spec] * 4 + [et_spec]
    per_dir_shapes = [rows(BF16)] * 4 + [et]
    return pl.pallas_call(
        functools.partial(_prep_kernel, tb=tb, nt=nt),
        grid=(bsz, nt),
        in_specs=[pl.BlockSpec((None, tb, PB_COLS), lambda b, t: (b, t, 0)),
                  pl.BlockSpec((None, 8, PB_COLS), lambda b, t: (b, jnp.maximum(t * hb - 1, 0), 0)),
                  pl.BlockSpec((None, 8, PB_COLS), lambda b, t: (b, jnp.minimum((t + 1) * hb, n8 - 1), 0))]
                 + [_const_spec(c.shape) for c in consts],
        out_specs=[row_spec] * 3 + per_dir_specs * 2,
        out_shape=[rows(BF16), rows(F32), rows(F32)] + per_dir_shapes * 2,
        compiler_params=_cp(2),
        name="rwkv_prep",
    )(pb, pb, pb, *consts)


def _pair_masks(reverse):
    t = lax.broadcasted_iota(jnp.int32, (CHUNK, PAIR), 0)
    s = lax.broadcasted_iota(jnp.int32, (CHUNK, PAIR), 1) % RW_N
    incl = (s >= t) if reverse else (s <= t)
    strict = (s > t) if reverse else (s < t)
    levels = []
    m = 1
    while m < CHUNK:
        if reverse:
            levels.append(((t // m) % 2 == 0) & (s // m == t // m + 1))
        else:
            levels.append(((t // m) % 2 == 1) & (s // m == t // m - 1))
        m *= 2
    return incl, strict, levels


def _dir_where(mask_f, mask_b, x):
    h = x.shape[0] // 2
    return jnp.concatenate([jnp.where(mask_f[None], x[:h], 0.0), jnp.where(mask_b[None], x[h:], 0.0)], axis=0)


def _block_diag(x):
    rows = lax.broadcasted_iota(jnp.int32, (2 * CHUNK, PAIR), 0) // CHUNK
    lanes = lax.broadcasted_iota(jnp.int32, (2 * CHUNK, PAIR), 1) // RW_N
    x = x.astype(BF16)
    return jnp.where((rows == lanes)[None], jnp.concatenate([x, x], axis=1), jnp.zeros((), BF16))


def _chunk_operators(kn, r, b, k, v, e_tot):
    incl_f, strict_f, levels_f = _pair_masks(False)
    incl_b, strict_b, levels_b = _pair_masks(True)
    mask2_f = jnp.concatenate([strict_f, incl_f], axis=0)
    mask2_b = jnp.concatenate([strict_b, incl_b], axis=0)
    t = lax.broadcasted_iota(jnp.int32, (CHUNK, PAIR), 0)
    s = lax.broadcasted_iota(jnp.int32, (CHUNK, PAIR), 1) % RW_N
    eye = (t == s).astype(F32)
    same_head = (lax.broadcasted_iota(jnp.int32, (PAIR, PAIR), 0) // RW_N
                 == lax.broadcasted_iota(jnp.int32, (PAIR, PAIR), 1) // RW_N)[None]
    xr = jnp.concatenate([kn, r], axis=1)
    ab = _dir_where(mask2_f, mask2_b, _bmm(xr, _block_diag(b), 2, 2))
    ak = _dir_where(mask2_f, mask2_b, _bmm(xr, _block_diag(k), 2, 2))
    av = _bmm(ak, _block_diag(v), 2, 1)
    a_b = ab[:, :CHUNK]
    a_rb = ab[:, CHUNK:]
    inv = eye[None] - _dir_where(levels_f[0], levels_b[0], a_b)
    for mk_f, mk_b in zip(levels_f[1:], levels_b[1:]):
        x = _bmm(_dir_where(mk_f, mk_b, a_b), _block_diag(inv), 2, 1)
        inv = inv - _bmm(inv, _block_diag(x), 2, 1)
    gw = _bmm(inv, jnp.concatenate([_block_diag(kn), _block_diag(av[:, :CHUNK])], axis=2), 2, 1)
    g = gw[:, :, :PAIR]
    w = gw[:, :, PAIR:]
    corr = _bmm(a_rb, jnp.concatenate([_block_diag(g), _block_diag(w)], axis=2), 2, 1)
    r_op = r.astype(F32) - corr[:, :, :PAIR]
    y0 = av[:, CHUNK:] - corr[:, :, PAIR:]
    p_op = jnp.where(same_head, _bmm(g, b, 1, 1), 0.0) * (-e_tot)
    vw = jnp.concatenate([v, w.astype(BF16)], axis=1)
    kb = jnp.concatenate([k, -b], axis=1)
    q_op = jnp.where(same_head, _bmm(vw, kb, 1, 1), 0.0) * e_tot
    return r_op, y0, p_op, q_op


def _scan_kernel(knf_ref, rf_ref, bf_ref, kf_ref, vf_ref, etf_ref,
                 knb_ref, rb_ref, bb_ref, kb_ref, vb_ref, etb_ref, yf_ref, yb_ref, s_ref, *, tb):
    @pl.when(pl.program_id(1) == 0)
    def _():
        s_ref[...] = jnp.zeros_like(s_ref)

    n_chunk = tb // CHUNK
    streams = ((knf_ref, rf_ref, bf_ref, kf_ref, vf_ref), (knb_ref, rb_ref, bb_ref, kb_ref, vb_ref))

    def stacked(ref_f, ref_b, rows):
        tiles = []
        for ref in (ref_f, ref_b):
            for c in range(n_chunk):
                x = ref[c * rows:(c + 1) * rows, :] if rows == CHUNK else ref[c]
                tiles += [x[:, p * PAIR:(p + 1) * PAIR] for p in range(N_PAIR)]
        return jnp.stack(tiles)

    ops = [stacked(streams[0][i], streams[1][i], CHUNK) for i in range(5)]
    e_tot = stacked(etf_ref, etb_ref, 1)
    r_op, y0, p_op, q_op = _chunk_operators(*ops, e_tot)

    state = s_ref[...]
    half = n_chunk * N_PAIR
    for i in range(n_chunk):
        cf, cb = i, n_chunk - 1 - i
        sel = lambda x: jnp.concatenate([x[cf * N_PAIR:(cf + 1) * N_PAIR],
                                         x[half + cb * N_PAIR:half + (cb + 1) * N_PAIR]], axis=0)
        y = _bmm(sel(r_op), state, 2, 2) + sel(y0)
        state = state * sel(e_tot) + _bmm(state, sel(p_op), 2, 1) + sel(q_op)
        for p in range(N_PAIR):
            yf_ref[cf * CHUNK:(cf + 1) * CHUNK, p * PAIR:(p + 1) * PAIR] = y[p]
            yb_ref[cb * CHUNK:(cb + 1) * CHUNK, p * PAIR:(p + 1) * PAIR] = y[N_PAIR + p]
    s_ref[...] = state


def _rwkv_scan(v, fwd_ops, bwd_ops, *, tb=256):
    bsz, t_len, _ = v.shape
    nt = t_len // tb
    n_chunk = tb // CHUNK

    def stream(tmap):
        rows = pl.BlockSpec((None, tb, RW_W), lambda bi, ti: (bi, tmap(ti), 0))
        et = pl.BlockSpec((None, n_chunk, 1, RW_W), lambda bi, ti: (bi, tmap(ti), 0, 0))
        return rows, [rows] * 5 + [et]

    rows_f, specs_f = stream(lambda t: t)
    rows_b, specs_b = stream(lambda t: nt - 1 - t)
    out_shape = jax.ShapeDtypeStruct(v.shape, F32)
    kn_f, r_f, b_f, k_f, et_f = fwd_ops
    kn_b, r_b, b_b, k_b, et_b = bwd_ops
    return pl.pallas_call(
        functools.partial(_scan_kernel, tb=tb),
        grid=(bsz, nt),
        in_specs=specs_f + specs_b,
        out_specs=[rows_f, rows_b],
        out_shape=[out_shape, out_shape],
        scratch_shapes=[pltpu.VMEM((2 * N_PAIR, PAIR, PAIR), F32)],
        compiler_params=_cp(2),
        name="rwkv_scan",
    )(kn_f, r_f, b_f, k_f, v, et_f, kn_b, r_b, b_b, k_b, v, et_b)


def _mixout_kernel(of_ref, ob_ref, g_ref, gn_ref, yf_ref, yb_ref, gate_ref, bonus_ref, lg_ref, lb_ref,
                   seg_ref, x_ref, w_ref, ng_ref, nb_ref, o_ref, *, n_sub):
    seg = seg_ref[...]
    sub = x_ref.shape[0] // n_sub
    tiles = [slice(i * sub, (i + 1) * sub) for i in range(n_sub)]
    ys = [yf_ref[rows, :] + yb_ref[rows, :] for rows in tiles]
    ycs = [y - _seg_sum(y, seg) * (1.0 / RW_N) for y in ys]
    variances = [_seg_sum(yc * yc, seg) * (1.0 / RW_N) for yc in ycs]
    outs = []
    for rows, yc, var in zip(tiles, ycs, variances):
        o = of_ref[rows, :] + ob_ref[rows, :]
        parts = []
        for h in range(GLA_HEADS):
            oh = o[:, h * GLA_DV:(h + 1) * GLA_DV]
            parts.append(oh * lax.rsqrt(jnp.mean(oh * oh, axis=-1, keepdims=True) + LN_EPS))
        g = g_ref[rows, :]
        o = jnp.concatenate(parts, axis=-1) * gn_ref[...] * (g * jax.nn.sigmoid(g))
        y = yc * lax.rsqrt(var + RW_GN_EPS) * lg_ref[...] + lb_ref[...]
        y = (y + bonus_ref[rows, :]) * gate_ref[rows, :]
        mixed = jnp.concatenate([o, y], axis=-1).astype(BF16)
        outs.append(jnp.dot(mixed, w_ref[...], preferred_element_type=F32))
    for rows, tm_out in zip(tiles, outs):
        o_ref[rows, :] = _layer_norm(ALPHA * x_ref[rows, :] + tm_out, ng_ref[...], nb_ref[...], LN_EPS)


def _mix_out(o_f, o_b, pa, gla_norm_g, y_f, y_b, gate, bonus, lnx_g, lnx_b, seg, x, w_out, ln_g, ln_b,
             *, tm=512):
    bsz, t_len, _ = x.shape
    row = lambda width: pl.BlockSpec((None, tm, width), lambda b, t: (b, t, 0))
    return pl.pallas_call(
        functools.partial(_mixout_kernel, n_sub=2),
        grid=(bsz, t_len // tm),
        in_specs=[row(GLA_W), row(GLA_W),
                  pl.BlockSpec((None, tm, GLA_W), lambda b, t: (b, t, 2)),
                  _const_spec((1, GLA_W)),
                  row(RW_W), row(RW_W), row(RW_W), row(RW_W),
                  _const_spec((1, RW_W)), _const_spec((1, RW_W)), _const_spec((RW_W, RW_W)),
                  row(D_MODEL),
                  _const_spec((D_MODEL, D_MODEL)),
                  _const_spec((1, D_MODEL)), _const_spec((1, D_MODEL))],
        out_specs=row(D_MODEL),
        out_shape=jax.ShapeDtypeStruct(x.shape, F32),
        compiler_params=_cp(2),
        name="mix_out",
    )(o_f, o_b, pa, gla_norm_g, y_f, y_b, gate, bonus, lnx_g, lnx_b, seg, x, w_out, ln_g, ln_b)


def _memkv_kernel(m_ref, g_ref, b_ref, w_ref, k_ref, v_ref):
    m = _layer_norm(m_ref[...], g_ref[...], b_ref[...], LN_EPS)
    kv = jnp.dot(m.astype(BF16), w_ref[...], preferred_element_type=F32)
    k_ref[...] = kv[:, :D_MODEL].astype(BF16)
    v_ref[...] = kv[:, D_MODEL:].astype(BF16)


def _mem_kv(mem, g, b, w_kv):
    bsz, m_tok, _ = mem.shape
    spec = pl.BlockSpec((None, m_tok, D_MODEL), lambda i: (i, 0, 0))
    shape = jax.ShapeDtypeStruct(mem.shape, BF16)
    return pl.pallas_call(
        _memkv_kernel,
        grid=(bsz,),
        in_specs=[spec, _const_spec((1, D_MODEL)), _const_spec((1, D_MODEL)),
                  _const_spec((D_MODEL, 2 * D_MODEL))],
        out_specs=[spec, spec],
        out_shape=[shape, shape],
        compiler_params=_cp(1),
        name="mem_kv",
    )(mem, g, b, w_kv)


def _ca_kernel(x_ref, k_ref, v_ref, wq_ref, wo_ref, g_ref, b_ref, o_ref, *, n_sub):
    k = k_ref[...]
    v = v_ref[...]
    sub = x_ref.shape[0] // n_sub
    heads = [slice(h * MEM_HD, (h + 1) * MEM_HD) for h in range(MEM_HEADS)]
    xs = [x_ref[i * sub:(i + 1) * sub, :] for i in range(n_sub)]
    qs = [jnp.dot(x.astype(BF16), wq_ref[...], preferred_element_type=F32).astype(BF16) for x in xs]
    scores = [[_mm(q[:, hs], k[:, hs], _NT) * (MEM_HD ** -0.5) for hs in heads] for q in qs]
    outs = []
    for i in range(n_sub):
        parts = []
        for s, hs in zip(scores[i], heads):
            e = jnp.exp(s - jnp.max(s, axis=-1, keepdims=True))
            p = e * (1.0 / jnp.sum(e, axis=-1, keepdims=True))
            parts.append(_mm(p, v[:, hs], _NN).astype(BF16))
        outs.append(jnp.dot(jnp.concatenate(parts, axis=-1), wo_ref[...], preferred_element_type=F32))
    for i in range(n_sub):
        o_ref[i * sub:(i + 1) * sub, :] = _layer_norm(ALPHA * xs[i] + outs[i], g_ref[...], b_ref[...], LN_EPS)


def _cross_attn(x, k, v, w_q, w_o, g, b, *, tm=512):
    bsz, t_len, _ = x.shape
    m_tok = k.shape[1]
    row = pl.BlockSpec((None, tm, D_MODEL), lambda bi, t: (bi, t, 0))
    mem = pl.BlockSpec((None, m_tok, D_MODEL), lambda bi, t: (bi, 0, 0))
    return pl.pallas_call(
        functools.partial(_ca_kernel, n_sub=2),
        grid=(bsz, t_len // tm),
        in_specs=[row, mem, mem, _const_spec((D_MODEL, D_MODEL)), _const_spec((D_MODEL, D_MODEL)),
                  _const_spec((1, D_MODEL)), _const_spec((1, D_MODEL))],
        out_specs=row,
        out_shape=jax.ShapeDtypeStruct(x.shape, F32),
        compiler_params=_cp(2),
        name="cross_attn",
    )(x, k, v, w_q, w_o, g, b)


def _pad_rows(w, start, total):
    return jnp.zeros((total, w.shape[1]), w.dtype).at[start:start + w.shape[0]].set(w)


def _layer(x, mem, p):
    bsz, t_len, _ = x.shape
    x2d = x.reshape(bsz * t_len, D_MODEL)
    x1 = _ffn_block(x2d, p["ffn1_w_in"], p["ffn1_w_out"], p["ln_ffn1_g"], p["ln_ffn1_b"])
    pa, pb = _in_proj(x1, p["w_pa"], p["w_pb"])
    pa = pa.reshape(bsz, t_len, PA_COLS)
    pb = pb.reshape(bsz, t_len, PB_COLS)
    o_f, o_b = _gla(pa, p["gla_up_f"], p["gla_b_f"], p["gla_up_b"], p["gla_b_b"])
    v, gate, bonus, *ops = _rwkv_prep(pb, p["rw"])
    y_f, y_b = _rwkv_scan(v, ops[:5], ops[5:])
    x2 = _mix_out(o_f, o_b, pa, p["gla_norm_g"], y_f, y_b, gate, bonus, p["lnx_g"], p["lnx_b"], p["rw"]["seg"],
                  x1.reshape(bsz, t_len, D_MODEL), p["w_out"], p["ln_mix_g"], p["ln_mix_b"])
    mk, mv = _mem_kv(mem, p["mem_ln_g"], p["mem_ln_b"], p["ca_w_kv"])
    x3 = _cross_attn(x2, mk, mv, p["ca_w_q"], p["ca_w_o"], p["ln_ca_g"], p["ln_ca_b"])
    x4 = _ffn_block(x3.reshape(bsz * t_len, D_MODEL), p["ffn2_w_in"], p["ffn2_w_out"],
                    p["ln_ffn2_g"], p["ln_ffn2_b"])
    return x4.reshape(bsz, t_len, D_MODEL)


def kernel(x_prompt, x_sample, mem_prompt, mem_sample, ffn1_w_in, ffn1_w_out, ln_ffn1_g, ln_ffn1_b, mix_w_in, gla_gate_up_fwd, gla_gate_b_fwd, gla_gate_up_bwd, gla_gate_b_bwd, gla_norm_g, rwkv_mu_prev, rwkv_mu_next, rwkv_w0_fwd, rwkv_w_up_fwd, rwkv_w0_bwd, rwkv_w_up_bwd, rwkv_a0, rwkv_a_up, rwkv_g_up, rwkv_k_k, rwkv_k_a, rwkv_r_k, rwkv_lnx_g, rwkv_lnx_b, mix_w_out, ln_mix_g, ln_mix_b, mem_ln_g, mem_ln_b, ca_w_q, ca_w_kv, ca_w_o, ln_ca_g, ln_ca_b, ffn2_w_in, ffn2_w_out, ln_ffn2_g, ln_ffn2_b):
    y_prompt, y_sample = x_prompt, x_sample
    for l in range(DEPTH):
        row = lambda a: a[l].reshape(1, -1)
        w_in = mix_w_in[l]
        zeros = lambda n: jnp.zeros((D_MODEL, n), F32)
        rw_off = GLA_COLS
        lr_off = rw_off + 3 * RW_W
        gd_off = lr_off + 2 * RW_DECAY_RANK + RW_AAA_RANK
        w_pa = jnp.concatenate([w_in[:, :GLA_COLS], zeros(PA_COLS - GLA_COLS)], axis=1)
        w_pb = jnp.concatenate([w_in[:, rw_off:gd_off], zeros(64), w_in[:, gd_off:]], axis=1)
        perm_mu = lambda mu: jnp.concatenate(
            [mu[:gd_off - rw_off], jnp.zeros((64,), F32), mu[gd_off - rw_off:]]).reshape(1, PB_COLS)
        rw = {
            "mu_prev": perm_mu(rwkv_mu_prev[l]), "mu_next": perm_mu(rwkv_mu_next[l]),
            "w0_f": row(rwkv_w0_fwd), "up_f": _pad_rows(rwkv_w_up_fwd[l], 0, 128).astype(BF16),
            "w0_b": row(rwkv_w0_bwd), "up_b": _pad_rows(rwkv_w_up_bwd[l], RW_DECAY_RANK, 128).astype(BF16),
            "a0": row(rwkv_a0), "a_up": _pad_rows(rwkv_a_up[l], 0, 128).astype(BF16),
            "g_up": rwkv_g_up[l].astype(BF16),
            "k_k": row(rwkv_k_k), "k_a": row(rwkv_k_a), "r_k": rwkv_r_k[l].reshape(1, RW_W),
            "seg": jnp.kron(jnp.eye(RW_HEADS, dtype=F32), jnp.ones((RW_N, RW_N), F32)).astype(BF16),
        }
        p = {
            "ffn1_w_in": ffn1_w_in[l].astype(BF16), "ffn1_w_out": ffn1_w_out[l].astype(BF16),
            "ln_ffn1_g": row(ln_ffn1_g), "ln_ffn1_b": row(ln_ffn1_b),
            "w_pa": w_pa.astype(BF16), "w_pb": w_pb.astype(BF16),
            "gla_up_f": _pad_rows(gla_gate_up_fwd[l], 0, 128), "gla_b_f": row(gla_gate_b_fwd),
            "gla_up_b": _pad_rows(gla_gate_up_bwd[l], GLA_RANK, 128), "gla_b_b": row(gla_gate_b_bwd),
            "gla_norm_g": row(gla_norm_g),
            "rw": rw,
            "lnx_g": row(rwkv_lnx_g), "lnx_b": row(rwkv_lnx_b),
            "w_out": mix_w_out[l].astype(BF16),
            "ln_mix_g": row(ln_mix_g), "ln_mix_b": row(ln_mix_b),
            "mem_ln_g": row(mem_ln_g), "mem_ln_b": row(mem_ln_b),
            "ca_w_q": ca_w_q[l].astype(BF16), "ca_w_kv": ca_w_kv[l].astype(BF16), "ca_w_o": ca_w_o[l].astype(BF16),
            "ln_ca_g": row(ln_ca_g), "ln_ca_b": row(ln_ca_b),
            "ffn2_w_in": ffn2_w_in[l].astype(BF16), "ffn2_w_out": ffn2_w_out[l].astype(BF16),
            "ln_ffn2_g": row(ln_ffn2_g), "ln_ffn2_b": row(ln_ffn2_b),
        }
        y_prompt = _layer(y_prompt, mem_prompt, p)
        y_sample = _layer(y_sample, mem_sample, p)
    return (y_prompt, y_sample)
```

```python
import functools

import jax
import jax.numpy as jnp
from jax import lax
from jax.experimental import pallas as pl
from jax.experimental.pallas import tpu as pltpu

F32 = jnp.float32
BF16 = jnp.bfloat16

D_MODEL = 1024
D_FF = 2816
DEPTH = 1
GLA_HEADS = 4
GLA_DK = 64
GLA_DV = 128
GLA_QK = GLA_HEADS * GLA_DK
GLA_W = GLA_HEADS * GLA_DV
GLA_RANK = 16
GLA_TAU = 16.0
RW_HEADS = 8
RW_N = 64
RW_W = RW_HEADS * RW_N
RW_DECAY_RANK = 64
RW_AAA_RANK = 64
RW_GATE_RANK = 128
RW_GN_EPS = 64e-5
GLA_COLS = 1568
RW_COLS = 1856
MEM_HEADS = 4
MEM_HD = D_MODEL // MEM_HEADS
LN_EPS = 1e-5
ALPHA = (2.0 * DEPTH) ** 0.25
CHUNK = 64
PAIR = 2 * RW_N
N_PAIR = RW_HEADS // 2
DECAY_SCALE = 0.6065306597126334

PA_COLS = 1664
PB_COLS = 1920

VMEM_LIMIT = 56 * 1024 * 1024


def _cp(n_axes):
    return pltpu.CompilerParams(dimension_semantics=("arbitrary",) * n_axes,
                                vmem_limit_bytes=VMEM_LIMIT)


def _mm(a, b, dims=((1,), (0,))):
    return lax.dot_general(a.astype(BF16), b.astype(BF16), (dims, ((), ())), preferred_element_type=F32)


_NN = ((1,), (0,))
_NT = ((1,), (1,))
_TN = ((0,), (0,))


def _bmm(a, b, ca, cb):
    return lax.dot_general(a.astype(BF16), b.astype(BF16), (((ca,), (cb,)), ((0,), (0,))),
                           preferred_element_type=F32)


def _layer_norm(z, g, b, eps):
    mu = jnp.mean(z, axis=-1, keepdims=True)
    zc = z - mu
    var = jnp.mean(zc * zc, axis=-1, keepdims=True)
    return zc * lax.rsqrt(var + eps) * g + b


def _softplus(x):
    return jnp.maximum(x, 0.0) + jnp.log(1.0 + jnp.exp(-jnp.abs(x)))


def _const_spec(shape):
    nd = len(shape)
    return pl.BlockSpec(shape, lambda *_: (0,) * nd, pipeline_mode=pl.Buffered(1))


def _ffn_kernel(x_ref, win_ref, wout_ref, g_ref, b_ref, o_ref, *, n_split, n_sub):
    fc = D_FF // n_split
    sub = x_ref.shape[0] // n_sub
    for i in range(n_sub):
        rows = slice(i * sub, (i + 1) * sub)
        x = x_ref[rows, :]
        xb = x.astype(BF16)
        acc = None
        for f in range(n_split):
            gate = jnp.dot(xb, win_ref[:, f * fc:(f + 1) * fc], preferred_element_type=F32)
            up = jnp.dot(xb, win_ref[:, D_FF + f * fc:D_FF + (f + 1) * fc], preferred_element_type=F32)
            h = (gate * jax.nn.sigmoid(gate) * up).astype(BF16)
            y = jnp.dot(h, wout_ref[f * fc:(f + 1) * fc, :], preferred_element_type=F32)
            acc = y if acc is None else acc + y
        o_ref[rows, :] = _layer_norm(ALPHA * x + 0.5 * acc, g_ref[...], b_ref[...], LN_EPS)


def _ffn_block(x, w_in, w_out, g, b, *, tm=1024, n_split=2, n_sub=2):
    n = x.shape[0]
    return pl.pallas_call(
        functools.partial(_ffn_kernel, n_split=n_split, n_sub=n_sub),
        grid=(n // tm,),
        in_specs=[pl.BlockSpec((tm, D_MODEL), lambda i: (i, 0)),
                  _const_spec((D_MODEL, 2 * D_FF)),
                  _const_spec((D_FF, D_MODEL)),
                  _const_spec((1, D_MODEL)),
                  _const_spec((1, D_MODEL))],
        out_specs=pl.BlockSpec((tm, D_MODEL), lambda i: (i, 0)),
        out_shape=jax.ShapeDtypeStruct((n, D_MODEL), F32),
        compiler_params=_cp(1),
        name="ffn_block",
    )(x, w_in, w_out, g, b)


def _inproj_kernel(x_ref, wa_ref, wb_ref, oa_ref, ob_ref):
    xb = x_ref[...].astype(BF16)
    oa_ref[...] = jnp.dot(xb, wa_ref[...], preferred_element_type=F32)
    ob_ref[...] = jnp.dot(xb, wb_ref[...], preferred_element_type=F32)


def _in_proj(x, wa, wb, *, tm=512):
    n = x.shape[0]
    return pl.pallas_call(
        _inproj_kernel,
        grid=(n // tm,),
        in_specs=[pl.BlockSpec((tm, D_MODEL), lambda i: (i, 0)),
                  _const_spec((D_MODEL, PA_COLS)),
                  _const_spec((D_MODEL, PB_COLS))],
        out_specs=[pl.BlockSpec((tm, PA_COLS), lambda i: (i, 0)),
                   pl.BlockSpec((tm, PB_COLS), lambda i: (i, 0))],
        out_shape=[jax.ShapeDtypeStruct((n, PA_COLS), F32),
                   jax.ShapeDtypeStruct((n, PB_COLS), F32)],
        compiler_params=_cp(1),
        name="in_proj",
    )(x, wa, wb)


def _tri_masks(reverse):
    row = lax.broadcasted_iota(jnp.int32, (CHUNK, CHUNK), 0)
    col = lax.broadcasted_iota(jnp.int32, (CHUNK, CHUNK), 1)
    incl = (col >= row) if reverse else (col <= row)
    strict = (col > row) if reverse else (col < row)
    return row, col, incl, strict


def _cum_consts(tb):
    idx = jnp.arange(tb)
    same = (idx[:, None] // CHUNK) == (idx[None, :] // CHUNK)
    fwd = same & (idx[None, :] <= idx[:, None])
    bwd = same & (idx[None, :] >= idx[:, None])
    return fwd.astype(BF16), bwd.astype(BF16)


def _mm01(m01, x):
    hi = x.astype(BF16)
    r1 = x - hi.astype(F32)
    mid = r1.astype(BF16)
    lo = (r1 - mid.astype(F32)).astype(BF16)
    dot = lambda p: jnp.dot(m01, p, preferred_element_type=F32)
    return dot(hi) + dot(mid) + dot(lo)


def _gla_kernel(qf_ref, kf_ref, vf_ref, gf_ref, qb_ref, kb_ref, vb_ref, gb_ref,
                upf_ref, bf_ref, upb_ref, bb_ref, cf_ref, cb_ref, of_ref, ob_ref, s_ref, *, tb):
    @pl.when(pl.program_id(1) == 0)
    def _():
        s_ref[...] = jnp.zeros_like(s_ref)

    n_chunk = tb // CHUNK
    n_pair = GLA_HEADS // 2
    pv = 2 * GLA_DV
    streams = ((qf_ref, kf_ref, vf_ref, gf_ref, upf_ref, bf_ref, cf_ref),
               (qb_ref, kb_ref, vb_ref, gb_ref, upb_ref, bb_ref, cb_ref))
    qt_t, kt_t, qs_t, ks_t, v_t, dec_t = [], [], [], [], [], []
    for d, (q_ref, k_ref, v_ref, g_ref, up_ref, gbias_ref, tri_ref) in enumerate(streams):
        z = _mm(g_ref[...], up_ref[...]) + gbias_ref[...]
        log_a = -_softplus(-z) / GLA_TAU
        b = _mm01(tri_ref[...], log_a)
        q = q_ref[...] * (GLA_DK ** -0.5)
        k = k_ref[...]
        v = v_ref[...].astype(BF16)
        for c in range(n_chunk):
            rows = slice(c * CHUNK, (c + 1) * CHUNK)
            end = c * CHUNK + (CHUNK - 1 if d == 0 else 0)
            half = 0.5 * b[end:end + 1, :]
            e_half = jnp.exp(half)
            qt = q[rows, :] * jnp.exp(b[rows, :] - half)
            kt = k[rows, :] * jnp.exp(half - b[rows, :])
            for p in range(n_pair):
                lanes = slice(p * PAIR, (p + 1) * PAIR)
                qt_t.append(qt[:, lanes])
                kt_t.append(kt[:, lanes])
                qs_t.append(qt[:, lanes] * e_half[:, lanes])
                ks_t.append(kt[:, lanes] * e_half[:, lanes])
                dec_t.append(e_half[:, lanes] * e_half[:, lanes])
                v_t.append(v[rows, p * pv:(p + 1) * pv])
    qt_s, kt_s, qs_s, ks_s, v_s = (jnp.stack(x) for x in (qt_t, kt_t, qs_t, ks_t, v_t))

    incl_f = _pair_masks(False)[0]
    incl_b = _pair_masks(True)[0]
    scores = _dir_where(incl_f, incl_b, _bmm(qt_s, _block_diag(kt_s), 2, 2))
    v_rows = lax.broadcasted_iota(jnp.int32, (PAIR, pv), 0) // CHUNK
    v_lanes = lax.broadcasted_iota(jnp.int32, (PAIR, pv), 1) // GLA_DV
    v_bd = jnp.where((v_rows == v_lanes)[None], jnp.concatenate([v_s, v_s], axis=1), jnp.zeros((), BF16))
    intra = _bmm(scores, v_bd, 2, 1)
    same_head = (lax.broadcasted_iota(jnp.int32, (pv, PAIR), 0) // GLA_DV
                 == lax.broadcasted_iota(jnp.int32, (pv, PAIR), 1) // GLA_DK)
    kv = jnp.where(same_head[None], _bmm(v_s, ks_s, 1, 1), 0.0)

    idx = lambda d, c, p: (d * n_chunk + c) * n_pair + p
    s_prev = [None] * (2 * n_chunk * n_pair)
    for d in range(2):
        for p in range(n_pair):
            state = s_ref[d * n_pair + p]
            for i in range(n_chunk):
                c = i if d == 0 else n_chunk - 1 - i
                s_prev[idx(d, c, p)] = state
                state = state * dec_t[idx(d, c, p)] + kv[idx(d, c, p)]
            s_ref[d * n_pair + p] = state
    out = intra + _bmm(qs_s, jnp.stack(s_prev), 2, 2)
    for d, o_ref in enumerate((of_ref, ob_ref)):
        for c in range(n_chunk):
            for p in range(n_pair):
                o_ref[c * CHUNK:(c + 1) * CHUNK, p * pv:(p + 1) * pv] = out[idx(d, c, p)]


def _gla(pa, up_f, bias_f, up_b, bias_b, *, tb=256):
    bsz, t_len, _ = pa.shape
    nt = t_len // tb
    cat_f, cat_b = _cum_consts(tb)

    def stream(tmap):
        return [pl.BlockSpec((None, tb, GLA_QK), lambda b, t: (b, tmap(t), 0)),
                pl.BlockSpec((None, tb, GLA_QK), lambda b, t: (b, tmap(t), 1)),
                pl.BlockSpec((None, tb, GLA_W), lambda b, t: (b, tmap(t), 1)),
                pl.BlockSpec((None, tb, 128), lambda b, t: (b, tmap(t), 12))]

    fwd = lambda t: t
    bwd = lambda t: nt - 1 - t
    out_shape = jax.ShapeDtypeStruct((bsz, t_len, GLA_W), F32)
    return pl.pallas_call(
        functools.partial(_gla_kernel, tb=tb),
        grid=(bsz, nt),
        in_specs=stream(fwd) + stream(bwd)
                 + [_const_spec((128, GLA_QK)), _const_spec((1, GLA_QK)),
                    _const_spec((128, GLA_QK)), _const_spec((1, GLA_QK)),
                    _const_spec((tb, tb)), _const_spec((tb, tb))],
        out_specs=[pl.BlockSpec((None, tb, GLA_W), lambda b, t: (b, fwd(t), 0)),
                   pl.BlockSpec((None, tb, GLA_W), lambda b, t: (b, bwd(t), 0))],
        out_shape=[out_shape, out_shape],
        scratch_shapes=[pltpu.VMEM((GLA_HEADS, 2 * GLA_DV, 2 * GLA_DK), F32)],
        compiler_params=_cp(2),
        name="gla",
    )(pa, pa, pa, pa, pa, pa, pa, pa, up_f, bias_f, up_b, bias_b, cat_f, cat_b)


def _seg_sum(x, seg):
    hi = x.astype(BF16)
    lo = (x - hi.astype(F32)).astype(BF16)
    return (jnp.dot(hi, seg, preferred_element_type=F32) + jnp.dot(lo, seg, preferred_element_type=F32))


def _prep_kernel(p_ref, hp_ref, hn_ref, mup_ref, mun_ref, w0f_ref, upf_ref, w0b_ref, upb_ref,
                 a0_ref, aup_ref, gup_ref, kk_ref, ka_ref, rk_ref, seg_ref, cf_ref, cb_ref,
                 v_o, gate_o, bonus_o, knf_o, rf_o, bf_o, kf_o, etf_o, knb_o, rb_o, bb_o, kb_o, etb_o,
                 *, tb, nt):
    t = pl.program_id(1)
    p = p_ref[...]
    row = lax.broadcasted_iota(jnp.int32, (tb, 1), 0)
    halo_prev = jnp.where(t > 0, hp_ref[7:8, :], 0.0)
    halo_next = jnp.where(t < nt - 1, hn_ref[0:1, :], 0.0)
    prev = jnp.where(row == 0, halo_prev, pltpu.roll(p, 1, 0))
    nxt = jnp.where(row == tb - 1, halo_next, pltpu.roll(p, tb - 1, 0))
    rw = p + mup_ref[...] * (prev - p) + mun_ref[...] * (nxt - p)

    r = rw[:, 0:512]
    kr = rw[:, 512:1024]
    vr = rw[:, 1024:1536]
    wd = jnp.tanh(rw[:, 1536:1664])
    ad = rw[:, 1664:1792]
    gd = jax.nn.sigmoid(rw[:, 1792:1920])

    a = jax.nn.sigmoid(a0_ref[...] + _mm(ad, aup_ref[...]))
    gate_o[...] = _mm(gd, gup_ref[...])
    seg = seg_ref[...]
    kk = kr * kk_ref[...]
    kn = kk * jnp.minimum(lax.rsqrt(_seg_sum(kk * kk, seg)), 1e12)
    k2 = kr * (1.0 + (a - 1.0) * ka_ref[...])
    bonus_o[...] = _seg_sum(r * k2 * rk_ref[...], seg) * vr
    v_o[...] = vr.astype(BF16)
    b = a * kn

    dirs = ((w0f_ref, upf_ref, cf_ref, knf_o, rf_o, bf_o, kf_o, etf_o),
            (w0b_ref, upb_ref, cb_ref, knb_o, rb_o, bb_o, kb_o, etb_o))
    for d, (w0_ref, up_ref, tri_ref, kn_o, r_o, b_o, k_o, et_o) in enumerate(dirs):
        lw = -DECAY_SCALE * jax.nn.sigmoid(w0_ref[...] + _mm(wd, up_ref[...]))
        cs = _mm01(tri_ref[...], lw)
        e_neg = jnp.exp(-cs)
        kn_o[...] = (kn * jnp.exp(cs - lw)).astype(BF16)
        r_o[...] = (r * jnp.exp(cs)).astype(BF16)
        b_o[...] = (b * e_neg).astype(BF16)
        k_o[...] = (k2 * e_neg).astype(BF16)
        for c in range(tb // CHUNK):
            end = c * CHUNK + (CHUNK - 1 if d == 0 else 0)
            et_o[c] = jnp.exp(cs[end:end + 1, :])


def _rwkv_prep(pb, prm, *, tb=256):
    bsz, t_len, _ = pb.shape
    nt = t_len // tb
    hb = tb // 8
    n8 = t_len // 8
    n_chunk = tb // CHUNK
    cat_f, cat_b = _cum_consts(tb)
    consts = [prm["mu_prev"], prm["mu_next"], prm["w0_f"], prm["up_f"], prm["w0_b"], prm["up_b"],
              prm["a0"], prm["a_up"], prm["g_up"], prm["k_k"], prm["k_a"], prm["r_k"], prm["seg"],
              cat_f, cat_b]
    row_spec = pl.BlockSpec((None, tb, RW_W), lambda b, t: (b, t, 0))
    rows = lambda dt: jax.ShapeDtypeStruct((bsz, t_len, RW_W), dt)
    et_spec = pl.BlockSpec((None, n_chunk, 1, RW_W), lambda b, t: (b, t, 0, 0))
    et = jax.ShapeDtypeStruct((bsz, t_len // CHUNK, 1, RW_W), F32)
    per_dir_specs = [row_spec] * 4 + [et_spec]
    per_dir_shapes = [rows(BF16)] * 4 + [et]
    return pl.pallas_call(
        functools.partial(_prep_kernel, tb=tb, nt=nt),
        grid=(bsz, nt),
        in_specs=[pl.BlockSpec((None, tb, PB_COLS), lambda b, t: (b, t, 0)),
                  pl.BlockSpec((None, 8, PB_COLS), lambda b, t: (b, jnp.maximum(t * hb - 1, 0), 0)),
                  pl.BlockSpec((None, 8, PB_COLS), lambda b, t: (b, jnp.minimum((t + 1) * hb, n8 - 1), 0))]
                 + [_const_spec(c.shape) for c in consts],
        out_specs=[row_spec] * 3 + per_dir_specs * 2,
        out_shape=[rows(BF16), rows(F32), rows(F32)] + per_dir_shapes * 2,
        compiler_params=_cp(2),
        name="rwkv_prep",
    )(pb, pb, pb, *consts)


def _pair_masks(reverse):
    t = lax.broadcasted_iota(jnp.int32, (CHUNK, PAIR), 0)
    s = lax.broadcasted_iota(jnp.int32, (CHUNK, PAIR), 1) % RW_N
    incl = (s >= t) if reverse else (s <= t)
    strict = (s > t) if reverse else (s < t)
    levels = []
    m = 1
    while m < CHUNK:
        if reverse:
            levels.append(((t // m) % 2 == 0) & (s // m == t // m + 1))
        else:
            levels.append(((t // m) % 2 == 1) & (s // m == t // m - 1))
        m *= 2
    return incl, strict, levels


def _dir_where(mask_f, mask_b, x):
    h = x.shape[0] // 2
    return jnp.concatenate([jnp.where(mask_f[None], x[:h], 0.0), jnp.where(mask_b[None], x[h:], 0.0)], axis=0)


def _block_diag(x):
    rows = lax.broadcasted_iota(jnp.int32, (2 * CHUNK, PAIR), 0) // CHUNK
    lanes = lax.broadcasted_iota(jnp.int32, (2 * CHUNK, PAIR), 1) // RW_N
    x = x.astype(BF16)
    return jnp.where((rows == lanes)[None], jnp.concatenate([x, x], axis=1), jnp.zeros((), BF16))


def _chunk_operators(kn, r, b, k, v, e_tot):
    incl_f, strict_f, levels_f = _pair_masks(False)
    incl_b, strict_b, levels_b = _pair_masks(True)
    mask2_f = jnp.concatenate([strict_f, incl_f], axis=0)
    mask2_b = jnp.concatenate([strict_b, incl_b], axis=0)
    t = lax.broadcasted_iota(jnp.int32, (CHUNK, PAIR), 0)
    s = lax.broadcasted_iota(jnp.int32, (CHUNK, PAIR), 1) % RW_N
    eye = (t == s).astype(F32)
    same_head = (lax.broadcasted_iota(jnp.int32, (PAIR, PAIR), 0) // RW_N
                 == lax.broadcasted_iota(jnp.int32, (PAIR, PAIR), 1) // RW_N)[None]
    xr = jnp.concatenate([kn, r], axis=1)
    ab = _dir_where(mask2_f, mask2_b, _bmm(xr, _block_diag(b), 2, 2))
    ak = _dir_where(mask2_f, mask2_b, _bmm(xr, _block_diag(k), 2, 2))
    av = _bmm(ak, _block_diag(v), 2, 1)
    a_b = ab[:, :CHUNK]
    a_rb = ab[:, CHUNK:]
    inv = eye[None] - _dir_where(levels_f[0], levels_b[0], a_b)
    for mk_f, mk_b in zip(levels_f[1:], levels_b[1:]):
        x = _bmm(_dir_where(mk_f, mk_b, a_b), _block_diag(inv), 2, 1)
        inv = inv - _bmm(inv, _block_diag(x), 2, 1)
    gw = _bmm(inv, jnp.concatenate([_block_diag(kn), _block_diag(av[:, :CHUNK])], axis=2), 2, 1)
    g = gw[:, :, :PAIR]
    w = gw[:, :, PAIR:]
    corr = _bmm(a_rb, jnp.concatenate([_block_diag(g), _block_diag(w)], axis=2), 2, 1)
    r_op = r.astype(F32) - corr[:, :, :PAIR]
    y0 = av[:, CHUNK:] - corr[:, :, PAIR:]
    p_op = jnp.where(same_head, _bmm(g, b, 1, 1), 0.0) * (-e_tot)
    vw = jnp.concatenate([v, w.astype(BF16)], axis=1)
    kb = jnp.concatenate([k, -b], axis=1)
    q_op = jnp.where(same_head, _bmm(vw, kb, 1, 1), 0.0) * e_tot
    return r_op, y0, p_op, q_op


def _scan_kernel(knf_ref, rf_ref, bf_ref, kf_ref, vf_ref, etf_ref,
                 knb_ref, rb_ref, bb_ref, kb_ref, vb_ref, etb_ref, yf_ref, yb_ref, s_ref, *, tb):
    @pl.when(pl.program_id(1) == 0)
    def _():
        s_ref[...] = jnp.zeros_like(s_ref)

    n_chunk = tb // CHUNK
    streams = ((knf_ref, rf_ref, bf_ref, kf_ref, vf_ref), (knb_ref, rb_ref, bb_ref, kb_ref, vb_ref))

    def stacked(ref_f, ref_b, rows):
        tiles = []
        for ref in (ref_f, ref_b):
            for c in range(n_chunk):
                x = ref[c * rows:(c + 1) * rows, :] if rows == CHUNK else ref[c]
                tiles += [x[:, p * PAIR:(p + 1) * PAIR] for p in range(N_PAIR)]
        return jnp.stack(tiles)

    ops = [stacked(streams[0][i], streams[1][i], CHUNK) for i in range(5)]
    e_tot = stacked(etf_ref, etb_ref, 1)
    r_op, y0, p_op, q_op = _chunk_operators(*ops, e_tot)

    state = s_ref[...]
    half = n_chunk * N_PAIR
    for i in range(n_chunk):
        cf, cb = i, n_chunk - 1 - i
        sel = lambda x: jnp.concatenate([x[cf * N_PAIR:(cf + 1) * N_PAIR],
                                         x[half + cb * N_PAIR:half + (cb + 1) * N_PAIR]], axis=0)
        y = _bmm(sel(r_op), state, 2, 2) + sel(y0)
        state = state * sel(e_tot) + _bmm(state, sel(p_op), 2, 1) + sel(q_op)
        for p in range(N_PAIR):
            yf_ref[cf * CHUNK:(cf + 1) * CHUNK, p * PAIR:(p + 1) * PAIR] = y[p]
            yb_ref[cb * CHUNK:(cb + 1) * CHUNK, p * PAIR:(p + 1) * PAIR] = y[N_PAIR + p]
    s_ref[...] = state


def _rwkv_scan(v, fwd_ops, bwd_ops, *, tb=512):
    bsz, t_len, _ = v.shape
    nt = t_len // tb
    n_chunk = tb // CHUNK

    def stream(tmap):
        rows = pl.BlockSpec((None, tb, RW_W), lambda bi, ti: (bi, tmap(ti), 0))
        et = pl.BlockSpec((None, n_chunk, 1, RW_W), lambda bi, ti: (bi, tmap(ti), 0, 0))
        return rows, [rows] * 5 + [et]

    rows_f, specs_f = stream(lambda t: t)
    rows_b, specs_b = stream(lambda t: nt - 1 - t)
    out_shape = jax.ShapeDtypeStruct(v.shape, F32)
    kn_f, r_f, b_f, k_f, et_f = fwd_ops
    kn_b, r_b, b_b, k_b, et_b = bwd_ops
    return pl.pallas_call(
        functools.partial(_scan_kernel, tb=tb),
        grid=(bsz, nt),
        in_specs=specs_f + specs_b,
        out_specs=[rows_f, rows_b],
        out_shape=[out_shape, out_shape],
        scratch_shapes=[pltpu.VMEM((2 * N_PAIR, PAIR, PAIR), F32)],
        compiler_params=_cp(2),
        name="rwkv_scan",
    )(kn_f, r_f, b_f, k_f, v, et_f, kn_b, r_b, b_b, k_b, v, et_b)


def _mixout_kernel(of_ref, ob_ref, g_ref, gn_ref, yf_ref, yb_ref, gate_ref, bonus_ref, lg_ref, lb_ref,
                   seg_ref, x_ref, w_ref, ng_ref, nb_ref, o_ref, *, n_sub):
    seg = seg_ref[...]
    sub = x_ref.shape[0] // n_sub
    tiles = [slice(i * sub, (i + 1) * sub) for i in range(n_sub)]
    ys = [yf_ref[rows, :] + yb_ref[rows, :] for rows in tiles]
    ycs = [y - _seg_sum(y, seg) * (1.0 / RW_N) for y in ys]
    variances = [_seg_sum(yc * yc, seg) * (1.0 / RW_N) for yc in ycs]
    outs = []
    for rows, yc, var in zip(tiles, ycs, variances):
        o = of_ref[rows, :] + ob_ref[rows, :]
        parts = []
        for h in range(GLA_HEADS):
            oh = o[:, h * GLA_DV:(h + 1) * GLA_DV]
            parts.append(oh * lax.rsqrt(jnp.mean(oh * oh, axis=-1, keepdims=True) + LN_EPS))
        g = g_ref[rows, :]
        o = jnp.concatenate(parts, axis=-1) * gn_ref[...] * (g * jax.nn.sigmoid(g))
        y = yc * lax.rsqrt(var + RW_GN_EPS) * lg_ref[...] + lb_ref[...]
        y = (y + bonus_ref[rows, :]) * gate_ref[rows, :]
        mixed = jnp.concatenate([o, y], axis=-1).astype(BF16)
        outs.append(jnp.dot(mixed, w_ref[...], preferred_element_type=F32))
    for rows, tm_out in zip(tiles, outs):
        o_ref[rows, :] = _layer_norm(ALPHA * x_ref[rows, :] + tm_out, ng_ref[...], nb_ref[...], LN_EPS)


def _mix_out(o_f, o_b, pa, gla_norm_g, y_f, y_b, gate, bonus, lnx_g, lnx_b, seg, x, w_out, ln_g, ln_b,
             *, tm=512):
    bsz, t_len, _ = x.shape
    row = lambda width: pl.BlockSpec((None, tm, width), lambda b, t: (b, t, 0))
    return pl.pallas_call(
        functools.partial(_mixout_kernel, n_sub=2),
        grid=(bsz, t_len // tm),
        in_specs=[row(GLA_W), row(GLA_W),
                  pl.BlockSpec((None, tm, GLA_W), lambda b, t: (b, t, 2)),
                  _const_spec((1, GLA_W)),
                  row(RW_W), row(RW_W), row(RW_W), row(RW_W),
                  _const_spec((1, RW_W)), _const_spec((1, RW_W)), _const_spec((RW_W, RW_W)),
                  row(D_MODEL),
                  _const_spec((D_MODEL, D_MODEL)),
                  _const_spec((1, D_MODEL)), _const_spec((1, D_MODEL))],
        out_specs=row(D_MODEL),
        out_shape=jax.ShapeDtypeStruct(x.shape, F32),
        compiler_params=_cp(2),
        name="mix_out",
    )(o_f, o_b, pa, gla_norm_g, y_f, y_b, gate, bonus, lnx_g, lnx_b, seg, x, w_out, ln_g, ln_b)


def _memkv_kernel(m_ref, g_ref, b_ref, w_ref, k_ref, v_ref):
    m = _layer_norm(m_ref[...], g_ref[...], b_ref[...], LN_EPS)
    kv = jnp.dot(m.astype(BF16), w_ref[...], preferred_element_type=F32)
    k_ref[...] = kv[:, :D_MODEL].astype(BF16)
    v_ref[...] = kv[:, D_MODEL:].astype(BF16)


def _mem_kv(mem, g, b, w_kv):
    bsz, m_tok, _ = mem.shape
    spec = pl.BlockSpec((None, m_tok, D_MODEL), lambda i: (i, 0, 0))
    shape = jax.ShapeDtypeStruct(mem.shape, BF16)
    return pl.pallas_call(
        _memkv_kernel,
        grid=(bsz,),
        in_specs=[spec, _const_spec((1, D_MODEL)), _const_spec((1, D_MODEL)),
                  _const_spec((D_MODEL, 2 * D_MODEL))],
        out_specs=[spec, spec],
        out_shape=[shape, shape],
        compiler_params=_cp(1),
        name="mem_kv",
    )(mem, g, b, w_kv)


def _ca_kernel(x_ref, k_ref, v_ref, wq_ref, wo_ref, g_ref, b_ref, o_ref, *, n_sub):
    k = k_ref[...]
    v = v_ref[...]
    sub = x_ref.shape[0] // n_sub
    heads = [slice(h * MEM_HD, (h + 1) * MEM_HD) for h in range(MEM_HEADS)]
    xs = [x_ref[i * sub:(i + 1) * sub, :] for i in range(n_sub)]
    qs = [jnp.dot(x.astype(BF16), wq_ref[...], preferred_element_type=F32).astype(BF16) for x in xs]
    scores = [[_mm(q[:, hs], k[:, hs], _NT) * (MEM_HD ** -0.5) for hs in heads] for q in qs]
    outs = []
    for i in range(n_sub):
        parts = []
        for s, hs in zip(scores[i], heads):
            e = jnp.exp(s - jnp.max(s, axis=-1, keepdims=True))
            p = e * (1.0 / jnp.sum(e, axis=-1, keepdims=True))
            parts.append(_mm(p, v[:, hs], _NN).astype(BF16))
        outs.append(jnp.dot(jnp.concatenate(parts, axis=-1), wo_ref[...], preferred_element_type=F32))
    for i in range(n_sub):
        o_ref[i * sub:(i + 1) * sub, :] = _layer_norm(ALPHA * xs[i] + outs[i], g_ref[...], b_ref[...], LN_EPS)


def _cross_attn(x, k, v, w_q, w_o, g, b, *, tm=512):
    bsz, t_len, _ = x.shape
    m_tok = k.shape[1]
    row = pl.BlockSpec((None, tm, D_MODEL), lambda bi, t: (bi, t, 0))
    mem = pl.BlockSpec((None, m_tok, D_MODEL), lambda bi, t: (bi, 0, 0))
    return pl.pallas_call(
        functools.partial(_ca_kernel, n_sub=2),
        grid=(bsz, t_len // tm),
        in_specs=[row, mem, mem, _const_spec((D_MODEL, D_MODEL)), _const_spec((D_MODEL, D_MODEL)),
                  _const_spec((1, D_MODEL)), _const_spec((1, D_MODEL))],
        out_specs=row,
        out_shape=jax.ShapeDtypeStruct(x.shape, F32),
        compiler_params=_cp(2),
        name="cross_attn",
    )(x, k, v, w_q, w_o, g, b)


def _pad_rows(w, start, total):
    return jnp.zeros((total, w.shape[1]), w.dtype).at[start:start + w.shape[0]].set(w)


def _layer(x, mem, p):
    bsz, t_len, _ = x.shape
    x2d = x.reshape(bsz * t_len, D_MODEL)
    x1 = _ffn_block(x2d, p["ffn1_w_in"], p["ffn1_w_out"], p["ln_ffn1_g"], p["ln_ffn1_b"])
    pa, pb = _in_proj(x1, p["w_pa"], p["w_pb"])
    pa = pa.reshape(bsz, t_len, PA_COLS)
    pb = pb.reshape(bsz, t_len, PB_COLS)
    o_f, o_b = _gla(pa, p["gla_up_f"], p["gla_b_f"], p["gla_up_b"], p["gla_b_b"])
    v, gate, bonus, *ops = _rwkv_prep(pb, p["rw"])
    y_f, y_b = _rwkv_scan(v, ops[:5], ops[5:])
    x2 = _mix_out(o_f, o_b, pa, p["gla_norm_g"], y_f, y_b, gate, bonus, p["lnx_g"], p["lnx_b"], p["rw"]["seg"],
                  x1.reshape(bsz, t_len, D_MODEL), p["w_out"], p["ln_mix_g"], p["ln_mix_b"])
    mk, mv = _mem_kv(mem, p["mem_ln_g"], p["mem_ln_b"], p["ca_w_kv"])
    x3 = _cross_attn(x2, mk, mv, p["ca_w_q"], p["ca_w_o"], p["ln_ca_g"], p["ln_ca_b"])
    x4 = _ffn_block(x3.reshape(bsz * t_len, D_MODEL), p["ffn2_w_in"], p["ffn2_w_out"],
                    p["ln_ffn2_g"], p["ln_ffn2_b"])
    return x4.reshape(bsz, t_len, D_MODEL)


def kernel(x_prompt, x_sample, mem_prompt, mem_sample, ffn1_w_in, ffn1_w_out, ln_ffn1_g, ln_ffn1_b, mix_w_in, gla_gate_up_fwd, gla_gate_b_fwd, gla_gate_up_bwd, gla_gate_b_bwd, gla_norm_g, rwkv_mu_prev, rwkv_mu_next, rwkv_w0_fwd, rwkv_w_up_fwd, rwkv_w0_bwd, rwkv_w_up_bwd, rwkv_a0, rwkv_a_up, rwkv_g_up, rwkv_k_k, rwkv_k_a, rwkv_r_k, rwkv_lnx_g, rwkv_lnx_b, mix_w_out, ln_mix_g, ln_mix_b, mem_ln_g, mem_ln_b, ca_w_q, ca_w_kv, ca_w_o, ln_ca_g, ln_ca_b, ffn2_w_in, ffn2_w_out, ln_ffn2_g, ln_ffn2_b):
    y_prompt, y_sample = x_prompt, x_sample
    for l in range(DEPTH):
        row = lambda a: a[l].reshape(1, -1)
        w_in = mix_w_in[l]
        zeros = lambda n: jnp.zeros((D_MODEL, n), F32)
        rw_off = GLA_COLS
        lr_off = rw_off + 3 * RW_W
        gd_off = lr_off + 2 * RW_DECAY_RANK + RW_AAA_RANK
        w_pa = jnp.concatenate([w_in[:, :GLA_COLS], zeros(PA_COLS - GLA_COLS)], axis=1)
        w_pb = jnp.concatenate([w_in[:, rw_off:gd_off], zeros(64), w_in[:, gd_off:]], axis=1)
        perm_mu = lambda mu: jnp.concatenate(
            [mu[:gd_off - rw_off], jnp.zeros((64,), F32), mu[gd_off - rw_off:]]).reshape(1, PB_COLS)
        rw = {
            "mu_prev": perm_mu(rwkv_mu_prev[l]), "mu_next": perm_mu(rwkv_mu_next[l]),
            "w0_f": row(rwkv_w0_fwd), "up_f": _pad_rows(rwkv_w_up_fwd[l], 0, 128).astype(BF16),
            "w0_b": row(rwkv_w0_bwd), "up_b": _pad_rows(rwkv_w_up_bwd[l], RW_DECAY_RANK, 128).astype(BF16),
            "a0": row(rwkv_a0), "a_up": _pad_rows(rwkv_a_up[l], 0, 128).astype(BF16),
            "g_up": rwkv_g_up[l].astype(BF16),
            "k_k": row(rwkv_k_k), "k_a": row(rwkv_k_a), "r_k": rwkv_r_k[l].reshape(1, RW_W),
            "seg": jnp.kron(jnp.eye(RW_HEADS, dtype=F32), jnp.ones((RW_N, RW_N), F32)).astype(BF16),
        }
        p = {
            "ffn1_w_in": ffn1_w_in[l].astype(BF16), "ffn1_w_out": ffn1_w_out[l].astype(BF16),
            "ln_ffn1_g": row(ln_ffn1_g), "ln_ffn1_b": row(ln_ffn1_b),
            "w_pa": w_pa.astype(BF16), "w_pb": w_pb.astype(BF16),
            "gla_up_f": _pad_rows(gla_gate_up_fwd[l], 0, 128), "gla_b_f": row(gla_gate_b_fwd),
            "gla_up_b": _pad_rows(gla_gate_up_bwd[l], GLA_RANK, 128), "gla_b_b": row(gla_gate_b_bwd),
            "gla_norm_g": row(gla_norm_g),
            "rw": rw,
            "lnx_g": row(rwkv_lnx_g), "lnx_b": row(rwkv_lnx_b),
            "w_out": mix_w_out[l].astype(BF16),
            "ln_mix_g": row(ln_mix_g), "ln_mix_b": row(ln_mix_b),
            "mem_ln_g": row(mem_ln_g), "mem_ln_b": row(mem_ln_b),
            "ca_w_q": ca_w_q[l].astype(BF16), "ca_w_kv": ca_w_kv[l].astype(BF16), "ca_w_o": ca_w_o[l].astype(BF16),
            "ln_ca_g": row(ln_ca_g), "ln_ca_b": row(ln_ca_b),
            "ffn2_w_in": ffn2_w_in[l].astype(BF16), "ffn2_w_out": ffn2_w_out[l].astype(BF16),
            "ln_ffn2_g": row(ln_ffn2_g), "ln_ffn2_b": row(ln_ffn2_b),
        }
        y_prompt = _layer(y_prompt, mem_prompt, p)
        y_sample = _layer(y_sample, mem_sample, p)
    return (y_prompt, y_sample)
```

```python
import functools

import jax
import jax.numpy as jnp
from jax import lax
from jax.experimental import pallas as pl
from jax.experimental.pallas import tpu as pltpu

F32 = jnp.float32
BF16 = jnp.bfloat16

D_MODEL = 1024
D_FF = 2816
DEPTH = 1
GLA_HEADS = 4
GLA_DK = 64
GLA_DV = 128
GLA_QK = GLA_HEADS * GLA_DK
GLA_W = GLA_HEADS * GLA_DV
GLA_RANK = 16
GLA_TAU = 16.0
RW_HEADS = 8
RW_N = 64
RW_W = RW_HEADS * RW_N
RW_DECAY_RANK = 64
RW_AAA_RANK = 64
RW_GATE_RANK = 128
RW_GN_EPS = 64e-5
GLA_COLS = 1568
RW_COLS = 1856
MEM_HEADS = 4
MEM_HD = D_MODEL // MEM_HEADS
LN_EPS = 1e-5
ALPHA = (2.0 * DEPTH) ** 0.25
CHUNK = 64
PAIR = 2 * RW_N
N_PAIR = RW_HEADS // 2
DECAY_SCALE = 0.6065306597126334

PA_COLS = 1664
PB_COLS = 1920

VMEM_LIMIT = 56 * 1024 * 1024


def _cp(n_axes):
    return pltpu.CompilerParams(dimension_semantics=("arbitrary",) * n_axes,
                                vmem_limit_bytes=VMEM_LIMIT)


def _mm(a, b, dims=((1,), (0,))):
    return lax.dot_general(a.astype(BF16), b.astype(BF16), (dims, ((), ())), preferred_element_type=F32)


_NN = ((1,), (0,))
_NT = ((1,), (1,))
_TN = ((0,), (0,))


def _bmm(a, b, ca, cb):
    return lax.dot_general(a.astype(BF16), b.astype(BF16), (((ca,), (cb,)), ((0,), (0,))),
                           preferred_element_type=F32)


def _layer_norm(z, g, b, eps):
    mu = jnp.mean(z, axis=-1, keepdims=True)
    zc = z - mu
    var = jnp.mean(zc * zc, axis=-1, keepdims=True)
    return zc * lax.rsqrt(var + eps) * g + b


def _softplus(x):
    return jnp.maximum(x, 0.0) + jnp.log(1.0 + jnp.exp(-jnp.abs(x)))


def _const_spec(shape):
    nd = len(shape)
    return pl.BlockSpec(shape, lambda *_: (0,) * nd, pipeline_mode=pl.Buffered(1))


def _ffn_kernel(x_ref, win_ref, wout_ref, g_ref, b_ref, o_ref, *, n_split, n_sub):
    fc = D_FF // n_split
    sub = x_ref.shape[0] // n_sub
    for i in range(n_sub):
        rows = slice(i * sub, (i + 1) * sub)
        x = x_ref[rows, :]
        xb = x.astype(BF16)
        acc = None
        for f in range(n_split):
            gate = jnp.dot(xb, win_ref[:, f * fc:(f + 1) * fc], preferred_element_type=F32)
            up = jnp.dot(xb, win_ref[:, D_FF + f * fc:D_FF + (f + 1) * fc], preferred_element_type=F32)
            h = (gate * jax.nn.sigmoid(gate) * up).astype(BF16)
            y = jnp.dot(h, wout_ref[f * fc:(f + 1) * fc, :], preferred_element_type=F32)
            acc = y if acc is None else acc + y
        o_ref[rows, :] = _layer_norm(ALPHA * x + 0.5 * acc, g_ref[...], b_ref[...], LN_EPS)


def _ffn_block(x, w_in, w_out, g, b, *, tm=1024, n_split=2, n_sub=2):
    n = x.shape[0]
    return pl.pallas_call(
        functools.partial(_ffn_kernel, n_split=n_split, n_sub=n_sub),
        grid=(n // tm,),
        in_specs=[pl.BlockSpec((tm, D_MODEL), lambda i: (i, 0)),
                  _const_spec((D_MODEL, 2 * D_FF)),
                  _const_spec((D_FF, D_MODEL)),
                  _const_spec((1, D_MODEL)),
                  _const_spec((1, D_MODEL))],
        out_specs=pl.BlockSpec((tm, D_MODEL), lambda i: (i, 0)),
        out_shape=jax.ShapeDtypeStruct((n, D_MODEL), F32),
        compiler_params=_cp(1),
        name="ffn_block",
    )(x, w_in, w_out, g, b)


def _inproj_kernel(x_ref, wa_ref, wb_ref, oa_ref, ob_ref):
    xb = x_ref[...].astype(BF16)
    oa_ref[...] = jnp.dot(xb, wa_ref[...], preferred_element_type=F32)
    ob_ref[...] = jnp.dot(xb, wb_ref[...], preferred_element_type=F32)


def _in_proj(x, wa, wb, *, tm=1024):
    n = x.shape[0]
    return pl.pallas_call(
        _inproj_kernel,
        grid=(n // tm,),
        in_specs=[pl.BlockSpec((tm, D_MODEL), lambda i: (i, 0)),
                  _const_spec((D_MODEL, PA_COLS)),
                  _const_spec((D_MODEL, PB_COLS))],
        out_specs=[pl.BlockSpec((tm, PA_COLS), lambda i: (i, 0)),
                   pl.BlockSpec((tm, PB_COLS), lambda i: (i, 0))],
        out_shape=[jax.ShapeDtypeStruct((n, PA_COLS), F32),
                   jax.ShapeDtypeStruct((n, PB_COLS), F32)],
        compiler_params=_cp(1),
        name="in_proj",
    )(x, wa, wb)


def _tri_masks(reverse):
    row = lax.broadcasted_iota(jnp.int32, (CHUNK, CHUNK), 0)
    col = lax.broadcasted_iota(jnp.int32, (CHUNK, CHUNK), 1)
    incl = (col >= row) if reverse else (col <= row)
    strict = (col > row) if reverse else (col < row)
    return row, col, incl, strict


def _cum_consts(tb):
    idx = jnp.arange(tb)
    same = (idx[:, None] // CHUNK) == (idx[None, :] // CHUNK)
    fwd = same & (idx[None, :] <= idx[:, None])
    bwd = same & (idx[None, :] >= idx[:, None])
    return fwd.astype(BF16), bwd.astype(BF16)


def _mm01(m01, x):
    hi = x.astype(BF16)
    r1 = x - hi.astype(F32)
    mid = r1.astype(BF16)
    lo = (r1 - mid.astype(F32)).astype(BF16)
    dot = lambda p: jnp.dot(m01, p, preferred_element_type=F32)
    return dot(hi) + dot(mid) + dot(lo)


def _gla_kernel(qf_ref, kf_ref, vf_ref, gf_ref, qb_ref, kb_ref, vb_ref, gb_ref,
                upf_ref, bf_ref, upb_ref, bb_ref, cf_ref, cb_ref, of_ref, ob_ref, s_ref, *, tb):
    @pl.when(pl.program_id(1) == 0)
    def _():
        s_ref[...] = jnp.zeros_like(s_ref)

    n_chunk = tb // CHUNK
    n_pair = GLA_HEADS // 2
    pv = 2 * GLA_DV
    streams = ((qf_ref, kf_ref, vf_ref, gf_ref, upf_ref, bf_ref, cf_ref),
               (qb_ref, kb_ref, vb_ref, gb_ref, upb_ref, bb_ref, cb_ref))
    qt_t, kt_t, qs_t, ks_t, v_t, dec_t = [], [], [], [], [], []
    for d, (q_ref, k_ref, v_ref, g_ref, up_ref, gbias_ref, tri_ref) in enumerate(streams):
        z = _mm(g_ref[...], up_ref[...]) + gbias_ref[...]
        log_a = -_softplus(-z) / GLA_TAU
        b = _mm01(tri_ref[...], log_a)
        q = q_ref[...] * (GLA_DK ** -0.5)
        k = k_ref[...]
        v = v_ref[...].astype(BF16)
        for c in range(n_chunk):
            rows = slice(c * CHUNK, (c + 1) * CHUNK)
            end = c * CHUNK + (CHUNK - 1 if d == 0 else 0)
            half = 0.5 * b[end:end + 1, :]
            e_half = jnp.exp(half)
            qt = q[rows, :] * jnp.exp(b[rows, :] - half)
            kt = k[rows, :] * jnp.exp(half - b[rows, :])
            for p in range(n_pair):
                lanes = slice(p * PAIR, (p + 1) * PAIR)
                qt_t.append(qt[:, lanes])
                kt_t.append(kt[:, lanes])
                qs_t.append(qt[:, lanes] * e_half[:, lanes])
                ks_t.append(kt[:, lanes] * e_half[:, lanes])
                dec_t.append(e_half[:, lanes] * e_half[:, lanes])
                v_t.append(v[rows, p * pv:(p + 1) * pv])
    qt_s, kt_s, qs_s, ks_s, v_s = (jnp.stack(x) for x in (qt_t, kt_t, qs_t, ks_t, v_t))

    incl_f = _pair_masks(False)[0]
    incl_b = _pair_masks(True)[0]
    scores = _dir_where(incl_f, incl_b, _bmm(qt_s, _block_diag(kt_s), 2, 2))
    v_rows = lax.broadcasted_iota(jnp.int32, (PAIR, pv), 0) // CHUNK
    v_lanes = lax.broadcasted_iota(jnp.int32, (PAIR, pv), 1) // GLA_DV
    v_bd = jnp.where((v_rows == v_lanes)[None], jnp.concatenate([v_s, v_s], axis=1), jnp.zeros((), BF16))
    intra = _bmm(scores, v_bd, 2, 1)
    same_head = (lax.broadcasted_iota(jnp.int32, (pv, PAIR), 0) // GLA_DV
                 == lax.broadcasted_iota(jnp.int32, (pv, PAIR), 1) // GLA_DK)
    kv = jnp.where(same_head[None], _bmm(v_s, ks_s, 1, 1), 0.0)

    idx = lambda d, c, p: (d * n_chunk + c) * n_pair + p
    s_prev = [None] * (2 * n_chunk * n_pair)
    for d in range(2):
        for p in range(n_pair):
            state = s_ref[d * n_pair + p]
            for i in range(n_chunk):
                c = i if d == 0 else n_chunk - 1 - i
                s_prev[idx(d, c, p)] = state
                state = state * dec_t[idx(d, c, p)] + kv[idx(d, c, p)]
            s_ref[d * n_pair + p] = state
    out = intra + _bmm(qs_s, jnp.stack(s_prev), 2, 2)
    for d, o_ref in enumerate((of_ref, ob_ref)):
        for c in range(n_chunk):
            for p in range(n_pair):
                o_ref[c * CHUNK:(c + 1) * CHUNK, p * pv:(p + 1) * pv] = out[idx(d, c, p)].astype(BF16)


def _gla(pa, up_f, bias_f, up_b, bias_b, *, tb=256):
    bsz, t_len, _ = pa.shape
    nt = t_len // tb
    cat_f, cat_b = _cum_consts(tb)

    def stream(tmap):
        return [pl.BlockSpec((None, tb, GLA_QK), lambda b, t: (b, tmap(t), 0)),
                pl.BlockSpec((None, tb, GLA_QK), lambda b, t: (b, tmap(t), 1)),
                pl.BlockSpec((None, tb, GLA_W), lambda b, t: (b, tmap(t), 1)),
                pl.BlockSpec((None, tb, 128), lambda b, t: (b, tmap(t), 12))]

    fwd = lambda t: t
    bwd = lambda t: nt - 1 - t
    out_shape = jax.ShapeDtypeStruct((bsz, t_len, GLA_W), BF16)
    return pl.pallas_call(
        functools.partial(_gla_kernel, tb=tb),
        grid=(bsz, nt),
        in_specs=stream(fwd) + stream(bwd)
                 + [_const_spec((128, GLA_QK)), _const_spec((1, GLA_QK)),
                    _const_spec((128, GLA_QK)), _const_spec((1, GLA_QK)),
                    _const_spec((tb, tb)), _const_spec((tb, tb))],
        out_specs=[pl.BlockSpec((None, tb, GLA_W), lambda b, t: (b, fwd(t), 0)),
                   pl.BlockSpec((None, tb, GLA_W), lambda b, t: (b, bwd(t), 0))],
        out_shape=[out_shape, out_shape],
        scratch_shapes=[pltpu.VMEM((GLA_HEADS, 2 * GLA_DV, 2 * GLA_DK), F32)],
        compiler_params=_cp(2),
        name="gla",
    )(pa, pa, pa, pa, pa, pa, pa, pa, up_f, bias_f, up_b, bias_b, cat_f, cat_b)


def _seg_sum(x, seg):
    hi = x.astype(BF16)
    lo = (x - hi.astype(F32)).astype(BF16)
    return (jnp.dot(hi, seg, preferred_element_type=F32) + jnp.dot(lo, seg, preferred_element_type=F32))


def _prep_kernel(p_ref, hp_ref, hn_ref, mup_ref, mun_ref, w0f_ref, upf_ref, w0b_ref, upb_ref,
                 a0_ref, aup_ref, gup_ref, kk_ref, ka_ref, rk_ref, seg_ref, cf_ref, cb_ref,
                 v_o, gate_o, bonus_o, knf_o, rf_o, bf_o, kf_o, etf_o, knb_o, rb_o, bb_o, kb_o, etb_o,
                 *, tb, nt):
    t = pl.program_id(1)
    p = p_ref[...]
    row = lax.broadcasted_iota(jnp.int32, (tb, 1), 0)
    halo_prev = jnp.where(t > 0, hp_ref[7:8, :], 0.0)
    halo_next = jnp.where(t < nt - 1, hn_ref[0:1, :], 0.0)
    prev = jnp.where(row == 0, halo_prev, pltpu.roll(p, 1, 0))
    nxt = jnp.where(row == tb - 1, halo_next, pltpu.roll(p, tb - 1, 0))
    rw = p + mup_ref[...] * (prev - p) + mun_ref[...] * (nxt - p)

    r = rw[:, 0:512]
    kr = rw[:, 512:1024]
    vr = rw[:, 1024:1536]
    wd = jnp.tanh(rw[:, 1536:1664])
    ad = rw[:, 1664:1792]
    gd = jax.nn.sigmoid(rw[:, 1792:1920])

    a = jax.nn.sigmoid(a0_ref[...] + _mm(ad, aup_ref[...]))
    gate_o[...] = _mm(gd, gup_ref[...]).astype(BF16)
    seg = seg_ref[...]
    kk = kr * kk_ref[...]
    kn = kk * jnp.minimum(lax.rsqrt(_seg_sum(kk * kk, seg)), 1e12)
    k2 = kr * (1.0 + (a - 1.0) * ka_ref[...])
    bonus_o[...] = (_seg_sum(r * k2 * rk_ref[...], seg) * vr).astype(BF16)
    v_o[...] = vr.astype(BF16)
    b = a * kn

    dirs = ((w0f_ref, upf_ref, cf_ref, knf_o, rf_o, bf_o, kf_o, etf_o),
            (w0b_ref, upb_ref, cb_ref, knb_o, rb_o, bb_o, kb_o, etb_o))
    for d, (w0_ref, up_ref, tri_ref, kn_o, r_o, b_o, k_o, et_o) in enumerate(dirs):
        lw = -DECAY_SCALE * jax.nn.sigmoid(w0_ref[...] + _mm(wd, up_ref[...]))
        cs = _mm01(tri_ref[...], lw)
        e_neg = jnp.exp(-cs)
        kn_o[...] = (kn * jnp.exp(cs - lw)).astype(BF16)
        r_o[...] = (r * jnp.exp(cs)).astype(BF16)
        b_o[...] = (b * e_neg).astype(BF16)
        k_o[...] = (k2 * e_neg).astype(BF16)
        for c in range(tb // CHUNK):
            end = c * CHUNK + (CHUNK - 1 if d == 0 else 0)
            et_o[c] = jnp.exp(cs[end:end + 1, :])


def _rwkv_prep(pb, prm, *, tb=256):
    bsz, t_len, _ = pb.shape
    nt = t_len // tb
    hb = tb // 8
    n8 = t_len // 8
    n_chunk = tb // CHUNK
    cat_f, cat_b = _cum_consts(tb)
    consts = [prm["mu_prev"], prm["mu_next"], prm["w0_f"], prm["up_f"], prm["w0_b"], prm["up_b"],
              prm["a0"], prm["a_up"], prm["g_up"], prm["k_k"], prm["k_a"], prm["r_k"], prm["seg"],
              cat_f, cat_b]
    row_spec = pl.BlockSpec((None, tb, RW_W), lambda b, t: (b, t, 0))
    rows = lambda dt: jax.ShapeDtypeStruct((bsz, t_len, RW_W), dt)
    et_spec = pl.BlockSpec((None, n_chunk, 1, RW_W), lambda b, t: (b, t, 0, 0))
    et = jax.ShapeDtypeStruct((bsz, t_len // CHUNK, 1, RW_W), F32)
    per_dir_specs = [row_spec] * 4 + [et_spec]
    per_dir_shapes = [rows(BF16)] * 4 + [et]
    return pl.pallas_call(
        functools.partial(_prep_kernel, tb=tb, nt=nt),
        grid=(bsz, nt),
        in_specs=[pl.BlockSpec((None, tb, PB_COLS), lambda b, t: (b, t, 0)),
                  pl.BlockSpec((None, 8, PB_COLS), lambda b, t: (b, jnp.maximum(t * hb - 1, 0), 0)),
                  pl.BlockSpec((None, 8, PB_COLS), lambda b, t: (b, jnp.minimum((t + 1) * hb, n8 - 1), 0))]
                 + [_const_spec(c.shape) for c in consts],
        out_specs=[row_spec] * 3 + per_dir_specs * 2,
        out_shape=[rows(BF16)] * 3 + per_dir_shapes * 2,
        compiler_params=_cp(2),
        name="rwkv_prep",
    )(pb, pb, pb, *consts)


def _pair_masks(reverse):
    t = lax.broadcasted_iota(jnp.int32, (CHUNK, PAIR), 0)
    s = lax.broadcasted_iota(jnp.int32, (CHUNK, PAIR), 1) % RW_N
    incl = (s >= t) if reverse else (s <= t)
    strict = (s > t) if reverse else (s < t)
    levels = []
    m = 1
    while m < CHUNK:
        if reverse:
            levels.append(((t // m) % 2 == 0) & (s // m == t // m + 1))
        else:
            levels.append(((t // m) % 2 == 1) & (s // m == t // m - 1))
        m *= 2
    return incl, strict, levels


def _dir_where(mask_f, mask_b, x):
    h = x.shape[0] // 2
    return jnp.concatenate([jnp.where(mask_f[None], x[:h], 0.0), jnp.where(mask_b[None], x[h:], 0.0)], axis=0)


def _block_diag(x):
    rows = lax.broadcasted_iota(jnp.int32, (2 * CHUNK, PAIR), 0) // CHUNK
    lanes = lax.broadcasted_iota(jnp.int32, (2 * CHUNK, PAIR), 1) // RW_N
    x = x.astype(BF16)
    return jnp.where((rows == lanes)[None], jnp.concatenate([x, x], axis=1), jnp.zeros((), BF16))


def _chunk_operators(kn, r, b, k, v, e_tot):
    incl_f, strict_f, levels_f = _pair_masks(False)
    incl_b, strict_b, levels_b = _pair_masks(True)
    mask2_f = jnp.concatenate([strict_f, incl_f], axis=0)
    mask2_b = jnp.concatenate([strict_b, incl_b], axis=0)
    t = lax.broadcasted_iota(jnp.int32, (CHUNK, PAIR), 0)
    s = lax.broadcasted_iota(jnp.int32, (CHUNK, PAIR), 1) % RW_N
    eye = (t == s).astype(F32)
    same_head = (lax.broadcasted_iota(jnp.int32, (PAIR, PAIR), 0) // RW_N
                 == lax.broadcasted_iota(jnp.int32, (PAIR, PAIR), 1) // RW_N)[None]
    xr = jnp.concatenate([kn, r], axis=1)
    bk = jnp.concatenate([_block_diag(b), _block_diag(k)], axis=1)
    abk = _dir_where(jnp.concatenate([mask2_f, mask2_f], axis=1), jnp.concatenate([mask2_b, mask2_b], axis=1),
                     _bmm(xr, bk, 2, 2))
    ab = abk[:, :, :PAIR]
    ak = abk[:, :, PAIR:]
    av = _bmm(ak, _block_diag(v), 2, 1)
    a_b = ab[:, :CHUNK]
    a_rb = ab[:, CHUNK:]
    inv = eye[None] - _dir_where(levels_f[0], levels_b[0], a_b)
    for mk_f, mk_b in zip(levels_f[1:], levels_b[1:]):
        x = _bmm(_dir_where(mk_f, mk_b, a_b), _block_diag(inv), 2, 1)
        inv = inv - _bmm(inv, _block_diag(x), 2, 1)
    gw = _bmm(inv, jnp.concatenate([_block_diag(kn), _block_diag(av[:, :CHUNK])], axis=2), 2, 1)
    g = gw[:, :, :PAIR]
    w = gw[:, :, PAIR:]
    corr = _bmm(a_rb, jnp.concatenate([_block_diag(g), _block_diag(w)], axis=2), 2, 1)
    r_op = r.astype(F32) - corr[:, :, :PAIR]
    y0 = av[:, CHUNK:] - corr[:, :, PAIR:]
    p_op = jnp.where(same_head, _bmm(g, b, 1, 1), 0.0) * (-e_tot)
    vw = jnp.concatenate([v, w.astype(BF16)], axis=1)
    kb = jnp.concatenate([k, -b], axis=1)
    q_op = jnp.where(same_head, _bmm(vw, kb, 1, 1), 0.0) * e_tot
    return r_op, y0, p_op, q_op


def _scan_kernel(knf_ref, rf_ref, bf_ref, kf_ref, vf_ref, etf_ref,
                 knb_ref, rb_ref, bb_ref, kb_ref, vb_ref, etb_ref, yf_ref, yb_ref, s_ref, *, tb):
    @pl.when(pl.program_id(1) == 0)
    def _():
        s_ref[...] = jnp.zeros_like(s_ref)

    n_chunk = tb // CHUNK
    streams = ((knf_ref, rf_ref, bf_ref, kf_ref, vf_ref), (knb_ref, rb_ref, bb_ref, kb_ref, vb_ref))

    def stacked(ref_f, ref_b, rows):
        tiles = []
        for ref in (ref_f, ref_b):
            for c in range(n_chunk):
                x = ref[c * rows:(c + 1) * rows, :] if rows == CHUNK else ref[c]
                tiles += [x[:, p * PAIR:(p + 1) * PAIR] for p in range(N_PAIR)]
        return jnp.stack(tiles)

    ops = [stacked(streams[0][i], streams[1][i], CHUNK) for i in range(5)]
    e_tot = stacked(etf_ref, etb_ref, 1)
    r_op, y0, p_op, q_op = _chunk_operators(*ops, e_tot)

    state = s_ref[...]
    half = n_chunk * N_PAIR
    for i in range(n_chunk):
        cf, cb = i, n_chunk - 1 - i
        sel = lambda x: jnp.concatenate([x[cf * N_PAIR:(cf + 1) * N_PAIR],
                                         x[half + cb * N_PAIR:half + (cb + 1) * N_PAIR]], axis=0)
        y = _bmm(sel(r_op), state, 2, 2) + sel(y0)
        state = state * sel(e_tot) + _bmm(state, sel(p_op), 2, 1) + sel(q_op)
        for p in range(N_PAIR):
            yf_ref[cf * CHUNK:(cf + 1) * CHUNK, p * PAIR:(p + 1) * PAIR] = y[p].astype(BF16)
            yb_ref[cb * CHUNK:(cb + 1) * CHUNK, p * PAIR:(p + 1) * PAIR] = y[N_PAIR + p].astype(BF16)
    s_ref[...] = state


def _rwkv_scan(v, fwd_ops, bwd_ops, *, tb=512):
    bsz, t_len, _ = v.shape
    nt = t_len // tb
    n_chunk = tb // CHUNK

    def stream(tmap):
        rows = pl.BlockSpec((None, tb, RW_W), lambda bi, ti: (bi, tmap(ti), 0))
        et = pl.BlockSpec((None, n_chunk, 1, RW_W), lambda bi, ti: (bi, tmap(ti), 0, 0))
        return rows, [rows] * 5 + [et]

    rows_f, specs_f = stream(lambda t: t)
    rows_b, specs_b = stream(lambda t: nt - 1 - t)
    out_shape = jax.ShapeDtypeStruct(v.shape, BF16)
    kn_f, r_f, b_f, k_f, et_f = fwd_ops
    kn_b, r_b, b_b, k_b, et_b = bwd_ops
    return pl.pallas_call(
        functools.partial(_scan_kernel, tb=tb),
        grid=(bsz, nt),
        in_specs=specs_f + specs_b,
        out_specs=[rows_f, rows_b],
        out_shape=[out_shape, out_shape],
        scratch_shapes=[pltpu.VMEM((2 * N_PAIR, PAIR, PAIR), F32)],
        compiler_params=_cp(2),
        name="rwkv_scan",
    )(kn_f, r_f, b_f, k_f, v, et_f, kn_b, r_b, b_b, k_b, v, et_b)


def _mixout_kernel(of_ref, ob_ref, g_ref, gn_ref, yf_ref, yb_ref, gate_ref, bonus_ref, lg_ref, lb_ref,
                   seg_ref, x_ref, w_ref, ng_ref, nb_ref, o_ref, *, n_sub):
    seg = seg_ref[...]
    sub = x_ref.shape[0] // n_sub
    tiles = [slice(i * sub, (i + 1) * sub) for i in range(n_sub)]
    f32 = lambda ref, rows: ref[rows, :].astype(F32)
    ys = [f32(yf_ref, rows) + f32(yb_ref, rows) for rows in tiles]
    ycs = [y - _seg_sum(y, seg) * (1.0 / RW_N) for y in ys]
    variances = [_seg_sum(yc * yc, seg) * (1.0 / RW_N) for yc in ycs]
    outs = []
    for rows, yc, var in zip(tiles, ycs, variances):
        o = f32(of_ref, rows) + f32(ob_ref, rows)
        parts = []
        for h in range(GLA_HEADS):
            oh = o[:, h * GLA_DV:(h + 1) * GLA_DV]
            parts.append(oh * lax.rsqrt(jnp.mean(oh * oh, axis=-1, keepdims=True) + LN_EPS))
        g = g_ref[rows, :]
        o = jnp.concatenate(parts, axis=-1) * gn_ref[...] * (g * jax.nn.sigmoid(g))
        y = yc * lax.rsqrt(var + RW_GN_EPS) * lg_ref[...] + lb_ref[...]
        y = (y + f32(bonus_ref, rows)) * f32(gate_ref, rows)
        mixed = jnp.concatenate([o, y], axis=-1).astype(BF16)
        outs.append(jnp.dot(mixed, w_ref[...], preferred_element_type=F32))
    for rows, tm_out in zip(tiles, outs):
        o_ref[rows, :] = _layer_norm(ALPHA * x_ref[rows, :] + tm_out, ng_ref[...], nb_ref[...], LN_EPS)


def _mix_out(o_f, o_b, pa, gla_norm_g, y_f, y_b, gate, bonus, lnx_g, lnx_b, seg, x, w_out, ln_g, ln_b,
             *, tm=512):
    bsz, t_len, _ = x.shape
    row = lambda width: pl.BlockSpec((None, tm, width), lambda b, t: (b, t, 0))
    return pl.pallas_call(
        functools.partial(_mixout_kernel, n_sub=2),
        grid=(bsz, t_len // tm),
        in_specs=[row(GLA_W), row(GLA_W),
                  pl.BlockSpec((None, tm, GLA_W), lambda b, t: (b, t, 2)),
                  _const_spec((1, GLA_W)),
                  row(RW_W), row(RW_W), row(RW_W), row(RW_W),
                  _const_spec((1, RW_W)), _const_spec((1, RW_W)), _const_spec((RW_W, RW_W)),
                  row(D_MODEL),
                  _const_spec((D_MODEL, D_MODEL)),
                  _const_spec((1, D_MODEL)), _const_spec((1, D_MODEL))],
        out_specs=row(D_MODEL),
        out_shape=jax.ShapeDtypeStruct(x.shape, F32),
        compiler_params=_cp(2),
        name="mix_out",
    )(o_f, o_b, pa, gla_norm_g, y_f, y_b, gate, bonus, lnx_g, lnx_b, seg, x, w_out, ln_g, ln_b)


def _memkv_kernel(m_ref, g_ref, b_ref, w_ref, k_ref, v_ref):
    m = _layer_norm(m_ref[...], g_ref[...], b_ref[...], LN_EPS)
    kv = jnp.dot(m.astype(BF16), w_ref[...], preferred_element_type=F32)
    k_ref[...] = kv[:, :D_MODEL].astype(BF16)
    v_ref[...] = kv[:, D_MODEL:].astype(BF16)


def _mem_kv(mem, g, b, w_kv):
    bsz, m_tok, _ = mem.shape
    spec = pl.BlockSpec((None, m_tok, D_MODEL), lambda i: (i, 0, 0))
    shape = jax.ShapeDtypeStruct(mem.shape, BF16)
    return pl.pallas_call(
        _memkv_kernel,
        grid=(bsz,),
        in_specs=[spec, _const_spec((1, D_MODEL)), _const_spec((1, D_MODEL)),
                  _const_spec((D_MODEL, 2 * D_MODEL))],
        out_specs=[spec, spec],
        out_shape=[shape, shape],
        compiler_params=_cp(1),
        name="mem_kv",
    )(mem, g, b, w_kv)


def _ca_kernel(x_ref, k_ref, v_ref, wq_ref, wo_ref, g_ref, b_ref, o_ref, *, n_sub):
    k = k_ref[...]
    v = v_ref[...]
    sub = x_ref.shape[0] // n_sub
    heads = [slice(h * MEM_HD, (h + 1) * MEM_HD) for h in range(MEM_HEADS)]
    xs = [x_ref[i * sub:(i + 1) * sub, :] for i in range(n_sub)]
    qs = [jnp.dot(x.astype(BF16), wq_ref[...], preferred_element_type=F32).astype(BF16) for x in xs]
    scores = [[_mm(q[:, hs], k[:, hs], _NT) * (MEM_HD ** -0.5) for hs in heads] for q in qs]
    outs = []
    for i in range(n_sub):
        parts = []
        for s, hs in zip(scores[i], heads):
            e = jnp.exp(s - jnp.max(s, axis=-1, keepdims=True))
            p = e * (1.0 / jnp.sum(e, axis=-1, keepdims=True))
            parts.append(_mm(p, v[:, hs], _NN).astype(BF16))
        outs.append(jnp.dot(jnp.concatenate(parts, axis=-1), wo_ref[...], preferred_element_type=F32))
    for i in range(n_sub):
        o_ref[i * sub:(i + 1) * sub, :] = _layer_norm(ALPHA * xs[i] + outs[i], g_ref[...], b_ref[...], LN_EPS)


def _cross_attn(x, k, v, w_q, w_o, g, b, *, tm=512):
    bsz, t_len, _ = x.shape
    m_tok = k.shape[1]
    row = pl.BlockSpec((None, tm, D_MODEL), lambda bi, t: (bi, t, 0))
    mem = pl.BlockSpec((None, m_tok, D_MODEL), lambda bi, t: (bi, 0, 0))
    return pl.pallas_call(
        functools.partial(_ca_kernel, n_sub=2),
        grid=(bsz, t_len // tm),
        in_specs=[row, mem, mem, _const_spec((D_MODEL, D_MODEL)), _const_spec((D_MODEL, D_MODEL)),
                  _const_spec((1, D_MODEL)), _const_spec((1, D_MODEL))],
        out_specs=row,
        out_shape=jax.ShapeDtypeStruct(x.shape, F32),
        compiler_params=_cp(2),
        name="cross_attn",
    )(x, k, v, w_q, w_o, g, b)


def _pad_rows(w, start, total):
    return jnp.zeros((total, w.shape[1]), w.dtype).at[start:start + w.shape[0]].set(w)


def _layer(x, mem, p):
    bsz, t_len, _ = x.shape
    x2d = x.reshape(bsz * t_len, D_MODEL)
    x1 = _ffn_block(x2d, p["ffn1_w_in"], p["ffn1_w_out"], p["ln_ffn1_g"], p["ln_ffn1_b"])
    pa, pb = _in_proj(x1, p["w_pa"], p["w_pb"])
    pa = pa.reshape(bsz, t_len, PA_COLS)
    pb = pb.reshape(bsz, t_len, PB_COLS)
    o_f, o_b = _gla(pa, p["gla_up_f"], p["gla_b_f"], p["gla_up_b"], p["gla_b_b"])
    v, gate, bonus, *ops = _rwkv_prep(pb, p["rw"])
    y_f, y_b = _rwkv_scan(v, ops[:5], ops[5:])
    x2 = _mix_out(o_f, o_b, pa, p["gla_norm_g"], y_f, y_b, gate, bonus, p["lnx_g"], p["lnx_b"], p["rw"]["seg"],
                  x1.reshape(bsz, t_len, D_MODEL), p["w_out"], p["ln_mix_g"], p["ln_mix_b"])
    mk, mv = _mem_kv(mem, p["mem_ln_g"], p["mem_ln_b"], p["ca_w_kv"])
    x3 = _cross_attn(x2, mk, mv, p["ca_w_q"], p["ca_w_o"], p["ln_ca_g"], p["ln_ca_b"])
    x4 = _ffn_block(x3.reshape(bsz * t_len, D_MODEL), p["ffn2_w_in"], p["ffn2_w_out"],
                    p["ln_ffn2_g"], p["ln_ffn2_b"])
    return x4.reshape(bsz, t_len, D_MODEL)


def kernel(x_prompt, x_sample, mem_prompt, mem_sample, ffn1_w_in, ffn1_w_out, ln_ffn1_g, ln_ffn1_b, mix_w_in, gla_gate_up_fwd, gla_gate_b_fwd, gla_gate_up_bwd, gla_gate_b_bwd, gla_norm_g, rwkv_mu_prev, rwkv_mu_next, rwkv_w0_fwd, rwkv_w_up_fwd, rwkv_w0_bwd, rwkv_w_up_bwd, rwkv_a0, rwkv_a_up, rwkv_g_up, rwkv_k_k, rwkv_k_a, rwkv_r_k, rwkv_lnx_g, rwkv_lnx_b, mix_w_out, ln_mix_g, ln_mix_b, mem_ln_g, mem_ln_b, ca_w_q, ca_w_kv, ca_w_o, ln_ca_g, ln_ca_b, ffn2_w_in, ffn2_w_out, ln_ffn2_g, ln_ffn2_b):
    y_prompt, y_sample = x_prompt, x_sample
    for l in range(DEPTH):
        row = lambda a: a[l].reshape(1, -1)
        w_in = mix_w_in[l]
        zeros = lambda n: jnp.zeros((D_MODEL, n), F32)
        rw_off = GLA_COLS
        lr_off = rw_off + 3 * RW_W
        gd_off = lr_off + 2 * RW_DECAY_RANK + RW_AAA_RANK
        w_pa = jnp.concatenate([w_in[:, :GLA_COLS], zeros(PA_COLS - GLA_COLS)], axis=1)
        w_pb = jnp.concatenate([w_in[:, rw_off:gd_off], zeros(64), w_in[:, gd_off:]], axis=1)
        perm_mu = lambda mu: jnp.concatenate(
            [mu[:gd_off - rw_off], jnp.zeros((64,), F32), mu[gd_off - rw_off:]]).reshape(1, PB_COLS)
        rw = {
            "mu_prev": perm_mu(rwkv_mu_prev[l]), "mu_next": perm_mu(rwkv_mu_next[l]),
            "w0_f": row(rwkv_w0_fwd), "up_f": _pad_rows(rwkv_w_up_fwd[l], 0, 128).astype(BF16),
            "w0_b": row(rwkv_w0_bwd), "up_b": _pad_rows(rwkv_w_up_bwd[l], RW_DECAY_RANK, 128).astype(BF16),
            "a0": row(rwkv_a0), "a_up": _pad_rows(rwkv_a_up[l], 0, 128).astype(BF16),
            "g_up": rwkv_g_up[l].astype(BF16),
            "k_k": row(rwkv_k_k), "k_a": row(rwkv_k_a), "r_k": rwkv_r_k[l].reshape(1, RW_W),
            "seg": jnp.kron(jnp.eye(RW_HEADS, dtype=F32), jnp.ones((RW_N, RW_N), F32)).astype(BF16),
        }
        p = {
            "ffn1_w_in": ffn1_w_in[l].astype(BF16), "ffn1_w_out": ffn1_w_out[l].astype(BF16),
            "ln_ffn1_g": row(ln_ffn1_g), "ln_ffn1_b": row(ln_ffn1_b),
            "w_pa": w_pa.astype(BF16), "w_pb": w_pb.astype(BF16),
            "gla_up_f": _pad_rows(gla_gate_up_fwd[l], 0, 128), "gla_b_f": row(gla_gate_b_fwd),
            "gla_up_b": _pad_rows(gla_gate_up_bwd[l], GLA_RANK, 128), "gla_b_b": row(gla_gate_b_bwd),
            "gla_norm_g": row(gla_norm_g),
            "rw": rw,
            "lnx_g": row(rwkv_lnx_g), "lnx_b": row(rwkv_lnx_b),
            "w_out": mix_w_out[l].astype(BF16),
            "ln_mix_g": row(ln_mix_g), "ln_mix_b": row(ln_mix_b),
            "mem_ln_g": row(mem_ln_g), "mem_ln_b": row(mem_ln_b),
            "ca_w_q": ca_w_q[l].astype(BF16), "ca_w_kv": ca_w_kv[l].astype(BF16), "ca_w_o": ca_w_o[l].astype(BF16),
            "ln_ca_g": row(ln_ca_g), "ln_ca_b": row(ln_ca_b),
            "ffn2_w_in": ffn2_w_in[l].astype(BF16), "ffn2_w_out": ffn2_w_out[l].astype(BF16),
            "ln_ffn2_g": row(ln_ffn2_g), "ln_ffn2_b": row(ln_ffn2_b),
        }
        y_prompt = _layer(y_prompt, mem_prompt, p)
        y_sample = _layer(y_sample, mem_sample, p)
    return (y_prompt, y_sample)
```

```python
import functools

import jax
import jax.numpy as jnp
from jax import lax
from jax.experimental import pallas as pl
from jax.experimental.pallas import tpu as pltpu

F32 = jnp.float32
BF16 = jnp.bfloat16

D_MODEL = 1024
D_FF = 2816
DEPTH = 1
GLA_HEADS = 4
GLA_DK = 64
GLA_DV = 128
GLA_QK = GLA_HEADS * GLA_DK
GLA_W = GLA_HEADS * GLA_DV
GLA_RANK = 16
GLA_TAU = 16.0
RW_HEADS = 8
RW_N = 64
RW_W = RW_HEADS * RW_N
RW_DECAY_RANK = 64
RW_AAA_RANK = 64
RW_GATE_RANK = 128
RW_GN_EPS = 64e-5
GLA_COLS = 1568
RW_COLS = 1856
MEM_HEADS = 4
MEM_HD = D_MODEL // MEM_HEADS
LN_EPS = 1e-5
ALPHA = (2.0 * DEPTH) ** 0.25
CHUNK = 64
PAIR = 2 * RW_N
N_PAIR = RW_HEADS // 2
DECAY_SCALE = 0.6065306597126334

PA_COLS = 1664
PB_COLS = 1920

VMEM_LIMIT = 56 * 1024 * 1024


def _cp(n_axes):
    return pltpu.CompilerParams(dimension_semantics=("arbitrary",) * n_axes,
                                vmem_limit_bytes=VMEM_LIMIT)


def _mm(a, b, dims=((1,), (0,))):
    return lax.dot_general(a.astype(BF16), b.astype(BF16), (dims, ((), ())), preferred_element_type=F32)


_NN = ((1,), (0,))
_NT = ((1,), (1,))
_TN = ((0,), (0,))


def _bmm(a, b, ca, cb):
    return lax.dot_general(a.astype(BF16), b.astype(BF16), (((ca,), (cb,)), ((0,), (0,))),
                           preferred_element_type=F32)


def _layer_norm(z, g, b, eps):
    mu = jnp.mean(z, axis=-1, keepdims=True)
    zc = z - mu
    var = jnp.mean(zc * zc, axis=-1, keepdims=True)
    return zc * lax.rsqrt(var + eps) * g + b


def _softplus(x):
    return jnp.maximum(x, 0.0) + jnp.log(1.0 + jnp.exp(-jnp.abs(x)))


def _const_spec(shape):
    nd = len(shape)
    return pl.BlockSpec(shape, lambda *_: (0,) * nd, pipeline_mode=pl.Buffered(1))


def _ffn_kernel(x_ref, win_ref, wout_ref, g_ref, b_ref, o_ref, *, n_split, n_sub):
    fc = D_FF // n_split
    sub = x_ref.shape[0] // n_sub
    for i in range(n_sub):
        rows = slice(i * sub, (i + 1) * sub)
        x = x_ref[rows, :]
        xb = x.astype(BF16)
        acc = None
        for f in range(n_split):
            gate = jnp.dot(xb, win_ref[:, f * fc:(f + 1) * fc], preferred_element_type=F32)
            up = jnp.dot(xb, win_ref[:, D_FF + f * fc:D_FF + (f + 1) * fc], preferred_element_type=F32)
            h = (gate * jax.nn.sigmoid(gate) * up).astype(BF16)
            y = jnp.dot(h, wout_ref[f * fc:(f + 1) * fc, :], preferred_element_type=F32)
            acc = y if acc is None else acc + y
        o_ref[rows, :] = _layer_norm(ALPHA * x + 0.5 * acc, g_ref[...], b_ref[...], LN_EPS)


def _ffn_block(x, w_in, w_out, g, b, *, tm=1024, n_split=2, n_sub=2):
    n = x.shape[0]
    return pl.pallas_call(
        functools.partial(_ffn_kernel, n_split=n_split, n_sub=n_sub),
        grid=(n // tm,),
        in_specs=[pl.BlockSpec((tm, D_MODEL), lambda i: (i, 0)),
                  _const_spec((D_MODEL, 2 * D_FF)),
                  _const_spec((D_FF, D_MODEL)),
                  _const_spec((1, D_MODEL)),
                  _const_spec((1, D_MODEL))],
        out_specs=pl.BlockSpec((tm, D_MODEL), lambda i: (i, 0)),
        out_shape=jax.ShapeDtypeStruct((n, D_MODEL), F32),
        compiler_params=_cp(1),
        name="ffn_block",
    )(x, w_in, w_out, g, b)


def _inproj_kernel(x_ref, wa_ref, oa_ref):
    oa_ref[...] = jnp.dot(x_ref[...].astype(BF16), wa_ref[...], preferred_element_type=F32)


def _in_proj(x, wa, *, tm=1024):
    n = x.shape[0]
    return pl.pallas_call(
        _inproj_kernel,
        grid=(n // tm,),
        in_specs=[pl.BlockSpec((tm, D_MODEL), lambda i: (i, 0)),
                  _const_spec((D_MODEL, PA_COLS))],
        out_specs=pl.BlockSpec((tm, PA_COLS), lambda i: (i, 0)),
        out_shape=jax.ShapeDtypeStruct((n, PA_COLS), F32),
        compiler_params=_cp(1),
        name="in_proj",
    )(x, wa)


def _tri_masks(reverse):
    row = lax.broadcasted_iota(jnp.int32, (CHUNK, CHUNK), 0)
    col = lax.broadcasted_iota(jnp.int32, (CHUNK, CHUNK), 1)
    incl = (col >= row) if reverse else (col <= row)
    strict = (col > row) if reverse else (col < row)
    return row, col, incl, strict


def _cum_consts(tb):
    idx = jnp.arange(tb)
    same = (idx[:, None] // CHUNK) == (idx[None, :] // CHUNK)
    fwd = same & (idx[None, :] <= idx[:, None])
    bwd = same & (idx[None, :] >= idx[:, None])
    return fwd.astype(BF16), bwd.astype(BF16)


def _mm01(m01, x):
    hi = x.astype(BF16)
    r1 = x - hi.astype(F32)
    mid = r1.astype(BF16)
    lo = (r1 - mid.astype(F32)).astype(BF16)
    dot = lambda p: jnp.dot(m01, p, preferred_element_type=F32)
    return dot(hi) + dot(mid) + dot(lo)


def _gla_kernel(qf_ref, kf_ref, vf_ref, gf_ref, qb_ref, kb_ref, vb_ref, gb_ref,
                upf_ref, bf_ref, upb_ref, bb_ref, cf_ref, cb_ref, of_ref, ob_ref, s_ref, *, tb):
    @pl.when(pl.program_id(1) == 0)
    def _():
        s_ref[...] = jnp.zeros_like(s_ref)

    n_chunk = tb // CHUNK
    n_pair = GLA_HEADS // 2
    pv = 2 * GLA_DV
    streams = ((qf_ref, kf_ref, vf_ref, gf_ref, upf_ref, bf_ref, cf_ref),
               (qb_ref, kb_ref, vb_ref, gb_ref, upb_ref, bb_ref, cb_ref))
    qt_t, kt_t, qs_t, ks_t, v_t, dec_t = [], [], [], [], [], []
    for d, (q_ref, k_ref, v_ref, g_ref, up_ref, gbias_ref, tri_ref) in enumerate(streams):
        z = _mm(g_ref[...], up_ref[...]) + gbias_ref[...]
        log_a = -_softplus(-z) / GLA_TAU
        b = _mm01(tri_ref[...], log_a)
        q = q_ref[...] * (GLA_DK ** -0.5)
        k = k_ref[...]
        v = v_ref[...].astype(BF16)
        for c in range(n_chunk):
            rows = slice(c * CHUNK, (c + 1) * CHUNK)
            end = c * CHUNK + (CHUNK - 1 if d == 0 else 0)
            half = 0.5 * b[end:end + 1, :]
            e_half = jnp.exp(half)
            qt = q[rows, :] * jnp.exp(b[rows, :] - half)
            kt = k[rows, :] * jnp.exp(half - b[rows, :])
            for p in range(n_pair):
                lanes = slice(p * PAIR, (p + 1) * PAIR)
                qt_t.append(qt[:, lanes])
                kt_t.append(kt[:, lanes])
                qs_t.append(qt[:, lanes] * e_half[:, lanes])
                ks_t.append(kt[:, lanes] * e_half[:, lanes])
                dec_t.append(e_half[:, lanes] * e_half[:, lanes])
                v_t.append(v[rows, p * pv:(p + 1) * pv])
    qt_s, kt_s, qs_s, ks_s, v_s = (jnp.stack(x) for x in (qt_t, kt_t, qs_t, ks_t, v_t))

    incl_f = _pair_masks(False)[0]
    incl_b = _pair_masks(True)[0]
    scores = _dir_where(incl_f, incl_b, _bmm(qt_s, _block_diag(kt_s), 2, 2))
    v_rows = lax.broadcasted_iota(jnp.int32, (PAIR, pv), 0) // CHUNK
    v_lanes = lax.broadcasted_iota(jnp.int32, (PAIR, pv), 1) // GLA_DV
    v_bd = jnp.where((v_rows == v_lanes)[None], jnp.concatenate([v_s, v_s], axis=1), jnp.zeros((), BF16))
    intra = _bmm(scores, v_bd, 2, 1)
    same_head = (lax.broadcasted_iota(jnp.int32, (pv, PAIR), 0) // GLA_DV
                 == lax.broadcasted_iota(jnp.int32, (pv, PAIR), 1) // GLA_DK)
    kv = jnp.where(same_head[None], _bmm(v_s, ks_s, 1, 1), 0.0)

    idx = lambda d, c, p: (d * n_chunk + c) * n_pair + p
    s_prev = [None] * (2 * n_chunk * n_pair)
    for d in range(2):
        for p in range(n_pair):
            state = s_ref[d * n_pair + p]
            for i in range(n_chunk):
                c = i if d == 0 else n_chunk - 1 - i
                s_prev[idx(d, c, p)] = state
                state = state * dec_t[idx(d, c, p)] + kv[idx(d, c, p)]
            s_ref[d * n_pair + p] = state
    out = intra + _bmm(qs_s, jnp.stack(s_prev), 2, 2)
    for d, o_ref in enumerate((of_ref, ob_ref)):
        for c in range(n_chunk):
            for p in range(n_pair):
                o_ref[c * CHUNK:(c + 1) * CHUNK, p * pv:(p + 1) * pv] = out[idx(d, c, p)].astype(BF16)


def _gla(pa, up_f, bias_f, up_b, bias_b, *, tb=256):
    bsz, t_len, _ = pa.shape
    nt = t_len // tb
    cat_f, cat_b = _cum_consts(tb)

    def stream(tmap):
        return [pl.BlockSpec((None, tb, GLA_QK), lambda b, t: (b, tmap(t), 0)),
                pl.BlockSpec((None, tb, GLA_QK), lambda b, t: (b, tmap(t), 1)),
                pl.BlockSpec((None, tb, GLA_W), lambda b, t: (b, tmap(t), 1)),
                pl.BlockSpec((None, tb, 128), lambda b, t: (b, tmap(t), 12))]

    fwd = lambda t: t
    bwd = lambda t: nt - 1 - t
    out_shape = jax.ShapeDtypeStruct((bsz, t_len, GLA_W), BF16)
    return pl.pallas_call(
        functools.partial(_gla_kernel, tb=tb),
        grid=(bsz, nt),
        in_specs=stream(fwd) + stream(bwd)
                 + [_const_spec((128, GLA_QK)), _const_spec((1, GLA_QK)),
                    _const_spec((128, GLA_QK)), _const_spec((1, GLA_QK)),
                    _const_spec((tb, tb)), _const_spec((tb, tb))],
        out_specs=[pl.BlockSpec((None, tb, GLA_W), lambda b, t: (b, fwd(t), 0)),
                   pl.BlockSpec((None, tb, GLA_W), lambda b, t: (b, bwd(t), 0))],
        out_shape=[out_shape, out_shape],
        scratch_shapes=[pltpu.VMEM((GLA_HEADS, 2 * GLA_DV, 2 * GLA_DK), F32)],
        compiler_params=_cp(2),
        name="gla",
    )(pa, pa, pa, pa, pa, pa, pa, pa, up_f, bias_f, up_b, bias_b, cat_f, cat_b)


def _seg_sum(x, seg):
    hi = x.astype(BF16)
    lo = (x - hi.astype(F32)).astype(BF16)
    w = seg.shape[0]
    halves = [jnp.dot(hi[:, c:c + w], seg, preferred_element_type=F32)
              + jnp.dot(lo[:, c:c + w], seg, preferred_element_type=F32) for c in range(0, x.shape[1], w)]
    return jnp.concatenate(halves, axis=1)


def _prep_kernel(x_ref, xp_ref, xn_ref, wpb_ref, mup_ref, mun_ref, w0f_ref, upf_ref, w0b_ref, upb_ref,
                 a0_ref, aup_ref, gup_ref, kk_ref, ka_ref, rk_ref, seg_ref, cf_ref, cb_ref,
                 v_o, gate_o, bonus_o, knf_o, rf_o, bf_o, kf_o, etf_o, knb_o, rb_o, bb_o, kb_o, etb_o,
                 *, tb, nt, sub):
    t = pl.program_id(1)
    halo_prev = jnp.where(t > 0, xp_ref[...], 0.0)
    halo_next = jnp.where(t < nt - 1, xn_ref[...], 0.0)
    x_ext = jnp.concatenate([halo_prev, x_ref[...], halo_next], axis=0).astype(BF16)
    starts = range(0, tb, sub)
    p_exts = [jnp.dot(x_ext[r0:r0 + sub + 16], wpb_ref[...], preferred_element_type=F32) for r0 in starts]
    seg = seg_ref[...]
    dirs = ((w0f_ref, upf_ref, cf_ref, knf_o, rf_o, bf_o, kf_o, etf_o),
            (w0b_ref, upb_ref, cb_ref, knb_o, rb_o, bb_o, kb_o, etb_o))
    for r0, p_ext in zip(starts, p_exts):
        rows = slice(r0, r0 + sub)
        p = p_ext[8:8 + sub]
        prev = p_ext[7:7 + sub]
        nxt = p_ext[9:9 + sub]
        rw = p + mup_ref[...] * (prev - p) + mun_ref[...] * (nxt - p)

        r = rw[:, 0:512]
        kr = rw[:, 512:1024]
        vr = rw[:, 1024:1536]
        wd = jnp.tanh(rw[:, 1536:1664])
        ad = rw[:, 1664:1792]
        gd = jax.nn.sigmoid(rw[:, 1792:1920])

        a = jax.nn.sigmoid(a0_ref[...] + _mm(ad, aup_ref[...]))
        gate_o[rows, :] = _mm(gd, gup_ref[...]).astype(BF16)
        kk = kr * kk_ref[...]
        kn = kk * jnp.minimum(lax.rsqrt(_seg_sum(kk * kk, seg)), 1e12)
        k2 = kr * (1.0 + (a - 1.0) * ka_ref[...])
        bonus_o[rows, :] = (_seg_sum(r * k2 * rk_ref[...], seg) * vr).astype(BF16)
        v_o[rows, :] = vr.astype(BF16)
        b = a * kn

        for d, (w0_ref, up_ref, tri_ref, kn_o, r_o, b_o, k_o, et_o) in enumerate(dirs):
            lw = -DECAY_SCALE * jax.nn.sigmoid(w0_ref[...] + _mm(wd, up_ref[...]))
            cs = _mm01(tri_ref[...], lw)
            e_neg = jnp.exp(-cs)
            kn_o[rows, :] = (kn * jnp.exp(cs - lw)).astype(BF16)
            r_o[rows, :] = (r * jnp.exp(cs)).astype(BF16)
            b_o[rows, :] = (b * e_neg).astype(BF16)
            k_o[rows, :] = (k2 * e_neg).astype(BF16)
            for c in range(sub // CHUNK):
                end = c * CHUNK + (CHUNK - 1 if d == 0 else 0)
                et_o[r0 // CHUNK + c] = jnp.exp(cs[end:end + 1, :])


def _rwkv_prep(x, prm, *, tb=512, sub=256):
    bsz, t_len, _ = x.shape
    nt = t_len // tb
    hb = tb // 8
    n8 = t_len // 8
    n_chunk = tb // CHUNK
    cat_f, cat_b = _cum_consts(sub)
    consts = [prm["w_pb"], prm["mu_prev"], prm["mu_next"], prm["w0_f"], prm["up_f"], prm["w0_b"], prm["up_b"],
              prm["a0"], prm["a_up"], prm["g_up"], prm["k_k"], prm["k_a"], prm["r_k"], prm["seg"],
              cat_f, cat_b]
    row_spec = pl.BlockSpec((None, tb, RW_W), lambda b, t: (b, t, 0))
    rows = lambda dt: jax.ShapeDtypeStruct((bsz, t_len, RW_W), dt)
    et_spec = pl.BlockSpec((None, n_chunk, 1, RW_W), lambda b, t: (b, t, 0, 0))
    et = jax.ShapeDtypeStruct((bsz, t_len // CHUNK, 1, RW_W), F32)
    per_dir_specs = [row_spec] * 4 + [et_spec]
    per_dir_shapes = [rows(BF16)] * 4 + [et]
    return pl.pallas_call(
        functools.partial(_prep_kernel, tb=tb, nt=nt, sub=sub),
        grid=(bsz, nt),
        in_specs=[pl.BlockSpec((None, tb, D_MODEL), lambda b, t: (b, t, 0)),
                  pl.BlockSpec((None, 8, D_MODEL), lambda b, t: (b, jnp.maximum(t * hb - 1, 0), 0)),
                  pl.BlockSpec((None, 8, D_MODEL), lambda b, t: (b, jnp.minimum((t + 1) * hb, n8 - 1), 0))]
                 + [_const_spec(c.shape) for c in consts],
        out_specs=[row_spec] * 3 + per_dir_specs * 2,
        out_shape=[rows(BF16)] * 3 + per_dir_shapes * 2,
        compiler_params=_cp(2),
        name="rwkv_prep",
    )(x, x, x, *consts)


def _pair_masks(reverse):
    t = lax.broadcasted_iota(jnp.int32, (CHUNK, PAIR), 0)
    s = lax.broadcasted_iota(jnp.int32, (CHUNK, PAIR), 1) % RW_N
    incl = (s >= t) if reverse else (s <= t)
    strict = (s > t) if reverse else (s < t)
    levels = []
    m = 1
    while m < CHUNK:
        if reverse:
            levels.append(((t // m) % 2 == 0) & (s // m == t // m + 1))
        else:
            levels.append(((t // m) % 2 == 1) & (s // m == t // m - 1))
        m *= 2
    return incl, strict, levels


def _dir_where(mask_f, mask_b, x):
    h = x.shape[0] // 2
    return jnp.concatenate([jnp.where(mask_f[None], x[:h], 0.0), jnp.where(mask_b[None], x[h:], 0.0)], axis=0)


def _block_diag(x):
    rows = lax.broadcasted_iota(jnp.int32, (2 * CHUNK, PAIR), 0) // CHUNK
    lanes = lax.broadcasted_iota(jnp.int32, (2 * CHUNK, PAIR), 1) // RW_N
    x = x.astype(BF16)
    return jnp.where((rows == lanes)[None], jnp.concatenate([x, x], axis=1), jnp.zeros((), BF16))


def _chunk_operators(kn, r, b, k, v, e_tot):
    incl_f, strict_f, levels_f = _pair_masks(False)
    incl_b, strict_b, levels_b = _pair_masks(True)
    mask2_f = jnp.concatenate([strict_f, incl_f], axis=0)
    mask2_b = jnp.concatenate([strict_b, incl_b], axis=0)
    t = lax.broadcasted_iota(jnp.int32, (CHUNK, PAIR), 0)
    s = lax.broadcasted_iota(jnp.int32, (CHUNK, PAIR), 1) % RW_N
    eye = (t == s).astype(F32)
    same_head = (lax.broadcasted_iota(jnp.int32, (PAIR, PAIR), 0) // RW_N
                 == lax.broadcasted_iota(jnp.int32, (PAIR, PAIR), 1) // RW_N)[None]
    xr = jnp.concatenate([kn, r], axis=1)
    bk = jnp.concatenate([_block_diag(b), _block_diag(k)], axis=1)
    abk = _dir_where(jnp.concatenate([mask2_f, mask2_f], axis=1), jnp.concatenate([mask2_b, mask2_b], axis=1),
                     _bmm(xr, bk, 2, 2))
    ab = abk[:, :, :PAIR]
    ak = abk[:, :, PAIR:]
    av = _bmm(ak, _block_diag(v), 2, 1)
    a_b = ab[:, :CHUNK]
    a_rb = ab[:, CHUNK:]
    inv = eye[None] - _dir_where(levels_f[0], levels_b[0], a_b)
    for mk_f, mk_b in zip(levels_f[1:], levels_b[1:]):
        x = _bmm(_dir_where(mk_f, mk_b, a_b), _block_diag(inv), 2, 1)
        inv = inv - _bmm(inv, _block_diag(x), 2, 1)
    gw = _bmm(inv, jnp.concatenate([_block_diag(kn), _block_diag(av[:, :CHUNK])], axis=2), 2, 1)
    g = gw[:, :, :PAIR]
    w = gw[:, :, PAIR:]
    corr = _bmm(a_rb, jnp.concatenate([_block_diag(g), _block_diag(w)], axis=2), 2, 1)
    r_op = r.astype(F32) - corr[:, :, :PAIR]
    y0 = av[:, CHUNK:] - corr[:, :, PAIR:]
    p_op = jnp.where(same_head, _bmm(g, b, 1, 1), 0.0) * (-e_tot)
    vw = jnp.concatenate([v, w.astype(BF16)], axis=1)
    kb = jnp.concatenate([k, -b], axis=1)
    q_op = jnp.where(same_head, _bmm(vw, kb, 1, 1), 0.0) * e_tot
    return r_op, y0, p_op, q_op


def _scan_kernel(knf_ref, rf_ref, bf_ref, kf_ref, vf_ref, etf_ref,
                 knb_ref, rb_ref, bb_ref, kb_ref, vb_ref, etb_ref, yf_ref, yb_ref, s_ref, *, tb):
    @pl.when(pl.program_id(1) == 0)
    def _():
        s_ref[...] = jnp.zeros_like(s_ref)

    n_chunk = tb // CHUNK
    streams = ((knf_ref, rf_ref, bf_ref, kf_ref, vf_ref), (knb_ref, rb_ref, bb_ref, kb_ref, vb_ref))

    def stacked(ref_f, ref_b, rows):
        tiles = []
        for ref in (ref_f, ref_b):
            for c in range(n_chunk):
                x = ref[c * rows:(c + 1) * rows, :] if rows == CHUNK else ref[c]
                tiles += [x[:, p * PAIR:(p + 1) * PAIR] for p in range(N_PAIR)]
        return jnp.stack(tiles)

    ops = [stacked(streams[0][i], streams[1][i], CHUNK) for i in range(5)]
    e_tot = stacked(etf_ref, etb_ref, 1)
    r_op, y0, p_op, q_op = _chunk_operators(*ops, e_tot)

    state = s_ref[...]
    half = n_chunk * N_PAIR
    for i in range(n_chunk):
        cf, cb = i, n_chunk - 1 - i
        sel = lambda x: jnp.concatenate([x[cf * N_PAIR:(cf + 1) * N_PAIR],
                                         x[half + cb * N_PAIR:half + (cb + 1) * N_PAIR]], axis=0)
        y = _bmm(sel(r_op), state, 2, 2) + sel(y0)
        state = state * sel(e_tot) + _bmm(state, sel(p_op), 2, 1) + sel(q_op)
        for p in range(N_PAIR):
            yf_ref[cf * CHUNK:(cf + 1) * CHUNK, p * PAIR:(p + 1) * PAIR] = y[p].astype(BF16)
            yb_ref[cb * CHUNK:(cb + 1) * CHUNK, p * PAIR:(p + 1) * PAIR] = y[N_PAIR + p].astype(BF16)
    s_ref[...] = state


def _rwkv_scan(v, fwd_ops, bwd_ops, *, tb=512):
    bsz, t_len, _ = v.shape
    nt = t_len // tb
    n_chunk = tb // CHUNK

    def stream(tmap):
        rows = pl.BlockSpec((None, tb, RW_W), lambda bi, ti: (bi, tmap(ti), 0))
        et = pl.BlockSpec((None, n_chunk, 1, RW_W), lambda bi, ti: (bi, tmap(ti), 0, 0))
        return rows, [rows] * 5 + [et]

    rows_f, specs_f = stream(lambda t: t)
    rows_b, specs_b = stream(lambda t: nt - 1 - t)
    out_shape = jax.ShapeDtypeStruct(v.shape, BF16)
    kn_f, r_f, b_f, k_f, et_f = fwd_ops
    kn_b, r_b, b_b, k_b, et_b = bwd_ops
    return pl.pallas_call(
        functools.partial(_scan_kernel, tb=tb),
        grid=(bsz, nt),
        in_specs=specs_f + specs_b,
        out_specs=[rows_f, rows_b],
        out_shape=[out_shape, out_shape],
        scratch_shapes=[pltpu.VMEM((2 * N_PAIR, PAIR, PAIR), F32)],
        compiler_params=_cp(2),
        name="rwkv_scan",
    )(kn_f, r_f, b_f, k_f, v, et_f, kn_b, r_b, b_b, k_b, v, et_b)


def _mixout_kernel(of_ref, ob_ref, g_ref, gn_ref, yf_ref, yb_ref, gate_ref, bonus_ref, lg_ref, lb_ref,
                   seg_ref, x_ref, w_ref, ng_ref, nb_ref, o_ref, *, n_sub):
    seg = seg_ref[...]
    sub = x_ref.shape[0] // n_sub
    tiles = [slice(i * sub, (i + 1) * sub) for i in range(n_sub)]
    f32 = lambda ref, rows: ref[rows, :].astype(F32)
    ys = [f32(yf_ref, rows) + f32(yb_ref, rows) for rows in tiles]
    ycs = [y - _seg_sum(y, seg) * (1.0 / RW_N) for y in ys]
    variances = [_seg_sum(yc * yc, seg) * (1.0 / RW_N) for yc in ycs]
    outs = []
    for rows, yc, var in zip(tiles, ycs, variances):
        o = f32(of_ref, rows) + f32(ob_ref, rows)
        parts = []
        for h in range(GLA_HEADS):
            oh = o[:, h * GLA_DV:(h + 1) * GLA_DV]
            parts.append(oh * lax.rsqrt(jnp.mean(oh * oh, axis=-1, keepdims=True) + LN_EPS))
        g = g_ref[rows, :]
        o = jnp.concatenate(parts, axis=-1) * gn_ref[...] * (g * jax.nn.sigmoid(g))
        y = yc * lax.rsqrt(var + RW_GN_EPS) * lg_ref[...] + lb_ref[...]
        y = (y + f32(bonus_ref, rows)) * f32(gate_ref, rows)
        mixed = jnp.concatenate([o, y], axis=-1).astype(BF16)
        outs.append(jnp.dot(mixed, w_ref[...], preferred_element_type=F32))
    for rows, tm_out in zip(tiles, outs):
        o_ref[rows, :] = _layer_norm(ALPHA * x_ref[rows, :] + tm_out, ng_ref[...], nb_ref[...], LN_EPS)


def _mix_out(o_f, o_b, pa, gla_norm_g, y_f, y_b, gate, bonus, lnx_g, lnx_b, seg, x, w_out, ln_g, ln_b,
             *, tm=1024):
    bsz, t_len, _ = x.shape
    row = lambda width: pl.BlockSpec((None, tm, width), lambda b, t: (b, t, 0))
    return pl.pallas_call(
        functools.partial(_mixout_kernel, n_sub=4),
        grid=(bsz, t_len // tm),
        in_specs=[row(GLA_W), row(GLA_W),
                  pl.BlockSpec((None, tm, GLA_W), lambda b, t: (b, t, 2)),
                  _const_spec((1, GLA_W)),
                  row(RW_W), row(RW_W), row(RW_W), row(RW_W),
                  _const_spec((1, RW_W)), _const_spec((1, RW_W)), _const_spec((RW_W // 2, RW_W // 2)),
                  row(D_MODEL),
                  _const_spec((D_MODEL, D_MODEL)),
                  _const_spec((1, D_MODEL)), _const_spec((1, D_MODEL))],
        out_specs=row(D_MODEL),
        out_shape=jax.ShapeDtypeStruct(x.shape, F32),
        compiler_params=_cp(2),
        name="mix_out",
    )(o_f, o_b, pa, gla_norm_g, y_f, y_b, gate, bonus, lnx_g, lnx_b, seg, x, w_out, ln_g, ln_b)


def _memkv_kernel(m_ref, g_ref, b_ref, w_ref, k_ref, v_ref):
    m = _layer_norm(m_ref[...], g_ref[...], b_ref[...], LN_EPS)
    kv = jnp.dot(m.astype(BF16), w_ref[...], preferred_element_type=F32)
    k_ref[...] = kv[:, :D_MODEL].astype(BF16)
    v_ref[...] = kv[:, D_MODEL:].astype(BF16)


def _mem_kv(mem, g, b, w_kv):
    bsz, m_tok, _ = mem.shape
    spec = pl.BlockSpec((None, m_tok, D_MODEL), lambda i: (i, 0, 0))
    shape = jax.ShapeDtypeStruct(mem.shape, BF16)
    return pl.pallas_call(
        _memkv_kernel,
        grid=(bsz,),
        in_specs=[spec, _const_spec((1, D_MODEL)), _const_spec((1, D_MODEL)),
                  _const_spec((D_MODEL, 2 * D_MODEL))],
        out_specs=[spec, spec],
        out_shape=[shape, shape],
        compiler_params=_cp(1),
        name="mem_kv",
    )(mem, g, b, w_kv)


def _ca_kernel(x_ref, k_ref, v_ref, wq_ref, wo_ref, g_ref, b_ref, o_ref, *, n_sub):
    k = k_ref[...]
    v = v_ref[...]
    sub = x_ref.shape[0] // n_sub
    heads = [slice(h * MEM_HD, (h + 1) * MEM_HD) for h in range(MEM_HEADS)]
    xs = [x_ref[i * sub:(i + 1) * sub, :] for i in range(n_sub)]
    qs = [jnp.dot(x.astype(BF16), wq_ref[...], preferred_element_type=F32).astype(BF16) for x in xs]
    scores = [[_mm(q[:, hs], k[:, hs], _NT) * (MEM_HD ** -0.5) for hs in heads] for q in qs]
    outs = []
    for i in range(n_sub):
        parts = []
        for s, hs in zip(scores[i], heads):
            e = jnp.exp(s - jnp.max(s, axis=-1, keepdims=True))
            p = e * (1.0 / jnp.sum(e, axis=-1, keepdims=True))
            parts.append(_mm(p, v[:, hs], _NN).astype(BF16))
        outs.append(jnp.dot(jnp.concatenate(parts, axis=-1), wo_ref[...], preferred_element_type=F32))
    for i in range(n_sub):
        o_ref[i * sub:(i + 1) * sub, :] = _layer_norm(ALPHA * xs[i] + outs[i], g_ref[...], b_ref[...], LN_EPS)


def _cross_attn(x, k, v, w_q, w_o, g, b, *, tm=1024):
    bsz, t_len, _ = x.shape
    m_tok = k.shape[1]
    row = pl.BlockSpec((None, tm, D_MODEL), lambda bi, t: (bi, t, 0))
    mem = pl.BlockSpec((None, m_tok, D_MODEL), lambda bi, t: (bi, 0, 0))
    return pl.pallas_call(
        functools.partial(_ca_kernel, n_sub=4),
        grid=(bsz, t_len // tm),
        in_specs=[row, mem, mem, _const_spec((D_MODEL, D_MODEL)), _const_spec((D_MODEL, D_MODEL)),
                  _const_spec((1, D_MODEL)), _const_spec((1, D_MODEL))],
        out_specs=row,
        out_shape=jax.ShapeDtypeStruct(x.shape, F32),
        compiler_params=_cp(2),
        name="cross_attn",
    )(x, k, v, w_q, w_o, g, b)


def _pad_rows(w, start, total):
    return jnp.zeros((total, w.shape[1]), w.dtype).at[start:start + w.shape[0]].set(w)


def _layer(x, mem, p):
    bsz, t_len, _ = x.shape
    x2d = x.reshape(bsz * t_len, D_MODEL)
    x1 = _ffn_block(x2d, p["ffn1_w_in"], p["ffn1_w_out"], p["ln_ffn1_g"], p["ln_ffn1_b"])
    x1_seq = x1.reshape(bsz, t_len, D_MODEL)
    pa = _in_proj(x1, p["w_pa"]).reshape(bsz, t_len, PA_COLS)
    o_f, o_b = _gla(pa, p["gla_up_f"], p["gla_b_f"], p["gla_up_b"], p["gla_b_b"])
    v, gate, bonus, *ops = _rwkv_prep(x1_seq, p["rw"])
    y_f, y_b = _rwkv_scan(v, ops[:5], ops[5:])
    x2 = _mix_out(o_f, o_b, pa, p["gla_norm_g"], y_f, y_b, gate, bonus, p["lnx_g"], p["lnx_b"], p["rw"]["seg"],
                  x1_seq, p["w_out"], p["ln_mix_g"], p["ln_mix_b"])
    mk, mv = _mem_kv(mem, p["mem_ln_g"], p["mem_ln_b"], p["ca_w_kv"])
    x3 = _cross_attn(x2, mk, mv, p["ca_w_q"], p["ca_w_o"], p["ln_ca_g"], p["ln_ca_b"])
    x4 = _ffn_block(x3.reshape(bsz * t_len, D_MODEL), p["ffn2_w_in"], p["ffn2_w_out"],
                    p["ln_ffn2_g"], p["ln_ffn2_b"])
    return x4.reshape(bsz, t_len, D_MODEL)


def kernel(x_prompt, x_sample, mem_prompt, mem_sample, ffn1_w_in, ffn1_w_out, ln_ffn1_g, ln_ffn1_b, mix_w_in, gla_gate_up_fwd, gla_gate_b_fwd, gla_gate_up_bwd, gla_gate_b_bwd, gla_norm_g, rwkv_mu_prev, rwkv_mu_next, rwkv_w0_fwd, rwkv_w_up_fwd, rwkv_w0_bwd, rwkv_w_up_bwd, rwkv_a0, rwkv_a_up, rwkv_g_up, rwkv_k_k, rwkv_k_a, rwkv_r_k, rwkv_lnx_g, rwkv_lnx_b, mix_w_out, ln_mix_g, ln_mix_b, mem_ln_g, mem_ln_b, ca_w_q, ca_w_kv, ca_w_o, ln_ca_g, ln_ca_b, ffn2_w_in, ffn2_w_out, ln_ffn2_g, ln_ffn2_b):
    y_prompt, y_sample = x_prompt, x_sample
    for l in range(DEPTH):
        row = lambda a: a[l].reshape(1, -1)
        w_in = mix_w_in[l]
        zeros = lambda n: jnp.zeros((D_MODEL, n), F32)
        rw_off = GLA_COLS
        lr_off = rw_off + 3 * RW_W
        gd_off = lr_off + 2 * RW_DECAY_RANK + RW_AAA_RANK
        w_pa = jnp.concatenate([w_in[:, :GLA_COLS], zeros(PA_COLS - GLA_COLS)], axis=1)
        w_pb = jnp.concatenate([w_in[:, rw_off:gd_off], zeros(64), w_in[:, gd_off:]], axis=1)
        perm_mu = lambda mu: jnp.concatenate(
            [mu[:gd_off - rw_off], jnp.zeros((64,), F32), mu[gd_off - rw_off:]]).reshape(1, PB_COLS)
        rw = {
            "w_pb": w_pb.astype(BF16),
            "mu_prev": perm_mu(rwkv_mu_prev[l]), "mu_next": perm_mu(rwkv_mu_next[l]),
            "w0_f": row(rwkv_w0_fwd), "up_f": _pad_rows(rwkv_w_up_fwd[l], 0, 128).astype(BF16),
            "w0_b": row(rwkv_w0_bwd), "up_b": _pad_rows(rwkv_w_up_bwd[l], RW_DECAY_RANK, 128).astype(BF16),
            "a0": row(rwkv_a0), "a_up": _pad_rows(rwkv_a_up[l], 0, 128).astype(BF16),
            "g_up": rwkv_g_up[l].astype(BF16),
            "k_k": row(rwkv_k_k), "k_a": row(rwkv_k_a), "r_k": rwkv_r_k[l].reshape(1, RW_W),
            "seg": jnp.kron(jnp.eye(RW_HEADS // 2, dtype=F32), jnp.ones((RW_N, RW_N), F32)).astype(BF16),
        }
        p = {
            "ffn1_w_in": ffn1_w_in[l].astype(BF16), "ffn1_w_out": ffn1_w_out[l].astype(BF16),
            "ln_ffn1_g": row(ln_ffn1_g), "ln_ffn1_b": row(ln_ffn1_b),
            "w_pa": w_pa.astype(BF16),
            "gla_up_f": _pad_rows(gla_gate_up_fwd[l], 0, 128), "gla_b_f": row(gla_gate_b_fwd),
            "gla_up_b": _pad_rows(gla_gate_up_bwd[l], GLA_RANK, 128), "gla_b_b": row(gla_gate_b_bwd),
            "gla_norm_g": row(gla_norm_g),
            "rw": rw,
            "lnx_g": row(rwkv_lnx_g), "lnx_b": row(rwkv_lnx_b),
            "w_out": mix_w_out[l].astype(BF16),
            "ln_mix_g": row(ln_mix_g), "ln_mix_b": row(ln_mix_b),
            "mem_ln_g": row(mem_ln_g), "mem_ln_b": row(mem_ln_b),
            "ca_w_q": ca_w_q[l].astype(BF16), "ca_w_kv": ca_w_kv[l].astype(BF16), "ca_w_o": ca_w_o[l].astype(BF16),
            "ln_ca_g": row(ln_ca_g), "ln_ca_b": row(ln_ca_b),
            "ffn2_w_in": ffn2_w_in[l].astype(BF16), "ffn2_w_out": ffn2_w_out[l].astype(BF16),
            "ln_ffn2_g": row(ln_ffn2_g), "ln_ffn2_b": row(ln_ffn2_b),
        }
        y_prompt = _layer(y_prompt, mem_prompt, p)
        y_sample = _layer(y_sample, mem_sample, p)
    return (y_prompt, y_sample)
```

```python
import functools

import jax
import jax.numpy as jnp
from jax import lax
from jax.experimental import pallas as pl
from jax.experimental.pallas import tpu as pltpu

F32 = jnp.float32
BF16 = jnp.bfloat16

D_MODEL = 1024
D_FF = 2816
DEPTH = 1
GLA_HEADS = 4
GLA_DK = 64
GLA_DV = 128
GLA_QK = GLA_HEADS * GLA_DK
GLA_W = GLA_HEADS * GLA_DV
GLA_RANK = 16
GLA_TAU = 16.0
RW_HEADS = 8
RW_N = 64
RW_W = RW_HEADS * RW_N
RW_DECAY_RANK = 64
RW_AAA_RANK = 64
RW_GATE_RANK = 128
RW_GN_EPS = 64e-5
GLA_COLS = 1568
RW_COLS = 1856
MEM_HEADS = 4
MEM_HD = D_MODEL // MEM_HEADS
LN_EPS = 1e-5
ALPHA = (2.0 * DEPTH) ** 0.25
CHUNK = 64
PAIR = 2 * RW_N
N_PAIR = RW_HEADS // 2
DECAY_SCALE = 0.6065306597126334

PA_COLS = 1664
PB_COLS = 1920

VMEM_LIMIT = 56 * 1024 * 1024


def _cp(n_axes):
    return pltpu.CompilerParams(dimension_semantics=("arbitrary",) * n_axes,
                                vmem_limit_bytes=VMEM_LIMIT)


def _mm(a, b, dims=((1,), (0,))):
    return lax.dot_general(a.astype(BF16), b.astype(BF16), (dims, ((), ())), preferred_element_type=F32)


_NN = ((1,), (0,))
_NT = ((1,), (1,))
_TN = ((0,), (0,))


def _bmm(a, b, ca, cb):
    return lax.dot_general(a.astype(BF16), b.astype(BF16), (((ca,), (cb,)), ((0,), (0,))),
                           preferred_element_type=F32)


def _layer_norm(z, g, b, eps):
    mu = jnp.mean(z, axis=-1, keepdims=True)
    zc = z - mu
    var = jnp.mean(zc * zc, axis=-1, keepdims=True)
    return zc * lax.rsqrt(var + eps) * g + b


def _softplus(x):
    return jnp.maximum(x, 0.0) + jnp.log(1.0 + jnp.exp(-jnp.abs(x)))


def _const_spec(shape):
    nd = len(shape)
    return pl.BlockSpec(shape, lambda *_: (0,) * nd, pipeline_mode=pl.Buffered(1))


def _ffn_kernel(x_ref, win_ref, wout_ref, g_ref, b_ref, o_ref, *, n_split, n_sub):
    fc = D_FF // n_split
    sub = x_ref.shape[0] // n_sub
    for i in range(n_sub):
        rows = slice(i * sub, (i + 1) * sub)
        x = x_ref[rows, :]
        xb = x.astype(BF16)
        acc = None
        for f in range(n_split):
            gate = jnp.dot(xb, win_ref[:, f * fc:(f + 1) * fc], preferred_element_type=F32)
            up = jnp.dot(xb, win_ref[:, D_FF + f * fc:D_FF + (f + 1) * fc], preferred_element_type=F32)
            h = (gate * jax.nn.sigmoid(gate) * up).astype(BF16)
            y = jnp.dot(h, wout_ref[f * fc:(f + 1) * fc, :], preferred_element_type=F32)
            acc = y if acc is None else acc + y
        o_ref[rows, :] = _layer_norm(ALPHA * x + 0.5 * acc, g_ref[...], b_ref[...], LN_EPS)


def _ffn_block(x, w_in, w_out, g, b, *, tm=1024, n_split=2, n_sub=2):
    n = x.shape[0]
    return pl.pallas_call(
        functools.partial(_ffn_kernel, n_split=n_split, n_sub=n_sub),
        grid=(n // tm,),
        in_specs=[pl.BlockSpec((tm, D_MODEL), lambda i: (i, 0)),
                  _const_spec((D_MODEL, 2 * D_FF)),
                  _const_spec((D_FF, D_MODEL)),
                  _const_spec((1, D_MODEL)),
                  _const_spec((1, D_MODEL))],
        out_specs=pl.BlockSpec((tm, D_MODEL), lambda i: (i, 0)),
        out_shape=jax.ShapeDtypeStruct((n, D_MODEL), F32),
        compiler_params=_cp(1),
        name="ffn_block",
    )(x, w_in, w_out, g, b)


def _inproj_kernel(x_ref, wa_ref, oa_ref):
    oa_ref[...] = jnp.dot(x_ref[...].astype(BF16), wa_ref[...], preferred_element_type=F32)


def _in_proj(x, wa, *, tm=1024):
    n = x.shape[0]
    return pl.pallas_call(
        _inproj_kernel,
        grid=(n // tm,),
        in_specs=[pl.BlockSpec((tm, D_MODEL), lambda i: (i, 0)),
                  _const_spec((D_MODEL, PA_COLS))],
        out_specs=pl.BlockSpec((tm, PA_COLS), lambda i: (i, 0)),
        out_shape=jax.ShapeDtypeStruct((n, PA_COLS), F32),
        compiler_params=_cp(1),
        name="in_proj",
    )(x, wa)


def _tri_masks(reverse):
    row = lax.broadcasted_iota(jnp.int32, (CHUNK, CHUNK), 0)
    col = lax.broadcasted_iota(jnp.int32, (CHUNK, CHUNK), 1)
    incl = (col >= row) if reverse else (col <= row)
    strict = (col > row) if reverse else (col < row)
    return row, col, incl, strict


def _cum_consts(tb):
    idx = jnp.arange(tb)
    same = (idx[:, None] // CHUNK) == (idx[None, :] // CHUNK)
    fwd = same & (idx[None, :] <= idx[:, None])
    bwd = same & (idx[None, :] >= idx[:, None])
    return fwd.astype(BF16), bwd.astype(BF16)


def _mm01(m01, x):
    hi = x.astype(BF16)
    r1 = x - hi.astype(F32)
    mid = r1.astype(BF16)
    lo = (r1 - mid.astype(F32)).astype(BF16)
    dot = lambda p: jnp.dot(m01, p, preferred_element_type=F32)
    return dot(hi) + dot(mid) + dot(lo)


def _gla_kernel(qf_ref, kf_ref, vf_ref, gf_ref, qb_ref, kb_ref, vb_ref, gb_ref,
                upf_ref, bf_ref, upb_ref, bb_ref, cf_ref, cb_ref, of_ref, ob_ref, s_ref, *, tb):
    @pl.when(pl.program_id(1) == 0)
    def _():
        s_ref[...] = jnp.zeros_like(s_ref)

    n_chunk = tb // CHUNK
    n_pair = GLA_HEADS // 2
    pv = 2 * GLA_DV
    streams = ((qf_ref, kf_ref, vf_ref, gf_ref, upf_ref, bf_ref, cf_ref),
               (qb_ref, kb_ref, vb_ref, gb_ref, upb_ref, bb_ref, cb_ref))
    qt_t, kt_t, qs_t, ks_t, v_t, dec_t = [], [], [], [], [], []
    for d, (q_ref, k_ref, v_ref, g_ref, up_ref, gbias_ref, tri_ref) in enumerate(streams):
        z = _mm(g_ref[...], up_ref[...]) + gbias_ref[...]
        log_a = -_softplus(-z) / GLA_TAU
        b = _mm01(tri_ref[...], log_a)
        q = q_ref[...] * (GLA_DK ** -0.5)
        k = k_ref[...]
        v = v_ref[...].astype(BF16)
        for c in range(n_chunk):
            rows = slice(c * CHUNK, (c + 1) * CHUNK)
            end = c * CHUNK + (CHUNK - 1 if d == 0 else 0)
            half = 0.5 * b[end:end + 1, :]
            e_half = jnp.exp(half)
            qt = q[rows, :] * jnp.exp(b[rows, :] - half)
            kt = k[rows, :] * jnp.exp(half - b[rows, :])
            for p in range(n_pair):
                lanes = slice(p * PAIR, (p + 1) * PAIR)
                qt_t.append(qt[:, lanes])
                kt_t.append(kt[:, lanes])
                qs_t.append(qt[:, lanes] * e_half[:, lanes])
                ks_t.append(kt[:, lanes] * e_half[:, lanes])
                dec_t.append(e_half[:, lanes] * e_half[:, lanes])
                v_t.append(v[rows, p * pv:(p + 1) * pv])
    qt_s, kt_s, qs_s, ks_s, v_s = (jnp.stack(x) for x in (qt_t, kt_t, qs_t, ks_t, v_t))

    incl_f = _pair_masks(False)[0]
    incl_b = _pair_masks(True)[0]
    scores = _dir_where(incl_f, incl_b, _bmm(qt_s, _block_diag(kt_s), 2, 2))
    v_rows = lax.broadcasted_iota(jnp.int32, (PAIR, pv), 0) // CHUNK
    v_lanes = lax.broadcasted_iota(jnp.int32, (PAIR, pv), 1) // GLA_DV
    v_bd = jnp.where((v_rows == v_lanes)[None], jnp.concatenate([v_s, v_s], axis=1), jnp.zeros((), BF16))
    intra = _bmm(scores, v_bd, 2, 1)
    same_head = (lax.broadcasted_iota(jnp.int32, (pv, PAIR), 0) // GLA_DV
                 == lax.broadcasted_iota(jnp.int32, (pv, PAIR), 1) // GLA_DK)
    kv = jnp.where(same_head[None], _bmm(v_s, ks_s, 1, 1), 0.0)

    idx = lambda d, c, p: (d * n_chunk + c) * n_pair + p
    s_prev = [None] * (2 * n_chunk * n_pair)
    for d in range(2):
        for p in range(n_pair):
            state = s_ref[d * n_pair + p]
            for i in range(n_chunk):
                c = i if d == 0 else n_chunk - 1 - i
                s_prev[idx(d, c, p)] = state
                state = state * dec_t[idx(d, c, p)] + kv[idx(d, c, p)]
            s_ref[d * n_pair + p] = state
    out = intra + _bmm(qs_s, jnp.stack(s_prev), 2, 2)
    for d, o_ref in enumerate((of_ref, ob_ref)):
        for c in range(n_chunk):
            for p in range(n_pair):
                o_ref[c * CHUNK:(c + 1) * CHUNK, p * pv:(p + 1) * pv] = out[idx(d, c, p)].astype(BF16)


def _gla(pa, up_f, bias_f, up_b, bias_b, *, tb=256):
    bsz, t_len, _ = pa.shape
    nt = t_len // tb
    cat_f, cat_b = _cum_consts(tb)

    def stream(tmap):
        return [pl.BlockSpec((None, tb, GLA_QK), lambda b, t: (b, tmap(t), 0)),
                pl.BlockSpec((None, tb, GLA_QK), lambda b, t: (b, tmap(t), 1)),
                pl.BlockSpec((None, tb, GLA_W), lambda b, t: (b, tmap(t), 1)),
                pl.BlockSpec((None, tb, 128), lambda b, t: (b, tmap(t), 12))]

    fwd = lambda t: t
    bwd = lambda t: nt - 1 - t
    out_shape = jax.ShapeDtypeStruct((bsz, t_len, GLA_W), BF16)
    return pl.pallas_call(
        functools.partial(_gla_kernel, tb=tb),
        grid=(bsz, nt),
        in_specs=stream(fwd) + stream(bwd)
                 + [_const_spec((128, GLA_QK)), _const_spec((1, GLA_QK)),
                    _const_spec((128, GLA_QK)), _const_spec((1, GLA_QK)),
                    _const_spec((tb, tb)), _const_spec((tb, tb))],
        out_specs=[pl.BlockSpec((None, tb, GLA_W), lambda b, t: (b, fwd(t), 0)),
                   pl.BlockSpec((None, tb, GLA_W), lambda b, t: (b, bwd(t), 0))],
        out_shape=[out_shape, out_shape],
        scratch_shapes=[pltpu.VMEM((GLA_HEADS, 2 * GLA_DV, 2 * GLA_DK), F32)],
        compiler_params=_cp(2),
        name="gla",
    )(pa, pa, pa, pa, pa, pa, pa, pa, up_f, bias_f, up_b, bias_b, cat_f, cat_b)


def _seg_sum(x, seg):
    hi = x.astype(BF16)
    lo = (x - hi.astype(F32)).astype(BF16)
    w = seg.shape[0]
    halves = [jnp.dot(hi[:, c:c + w], seg, preferred_element_type=F32)
              + jnp.dot(lo[:, c:c + w], seg, preferred_element_type=F32) for c in range(0, x.shape[1], w)]
    return jnp.concatenate(halves, axis=1)


def _prep_kernel(x_ref, xp_ref, xn_ref, wpb_ref, mup_ref, mun_ref, w0f_ref, upf_ref, w0b_ref, upb_ref,
                 a0_ref, aup_ref, gup_ref, kk_ref, ka_ref, rk_ref, seg_ref, cf_ref, cb_ref,
                 v_o, gate_o, bonus_o, knf_o, rf_o, bf_o, kf_o, etf_o, knb_o, rb_o, bb_o, kb_o, etb_o,
                 *, tb, nt, sub):
    t = pl.program_id(1)
    halo_prev = jnp.where(t > 0, xp_ref[...], 0.0)
    halo_next = jnp.where(t < nt - 1, xn_ref[...], 0.0)
    x_ext = jnp.concatenate([halo_prev, x_ref[...], halo_next], axis=0).astype(BF16)
    starts = range(0, tb, sub)
    p_exts = [jnp.dot(x_ext[r0:r0 + sub + 16], wpb_ref[...], preferred_element_type=F32) for r0 in starts]
    seg = seg_ref[...]
    dirs = ((w0f_ref, upf_ref, cf_ref, knf_o, rf_o, bf_o, kf_o, etf_o),
            (w0b_ref, upb_ref, cb_ref, knb_o, rb_o, bb_o, kb_o, etb_o))
    rws = []
    for p_ext in p_exts:
        p = p_ext[8:8 + sub]
        rws.append(p + mup_ref[...] * (p_ext[7:7 + sub] - p) + mun_ref[...] * (p_ext[9:9 + sub] - p))
    rs = [rw[:, 0:512] for rw in rws]
    krs = [rw[:, 512:1024] for rw in rws]
    vrs = [rw[:, 1024:1536] for rw in rws]
    wds = [jnp.tanh(rw[:, 1536:1664]) for rw in rws]
    a_pre = [_mm(rw[:, 1664:1792], aup_ref[...]) for rw in rws]
    gates = [_mm(jax.nn.sigmoid(rw[:, 1792:1920]), gup_ref[...]) for rw in rws]
    lw_pre = [[_mm(wd, dirs[d][1][...]) for wd in wds] for d in range(2)]
    kks = [kr * kk_ref[...] for kr in krs]
    kk_ss = [_seg_sum(kk * kk, seg) for kk in kks]
    a_s = [jax.nn.sigmoid(a0_ref[...] + ap) for ap in a_pre]
    k2s = [kr * (1.0 + (a - 1.0) * ka_ref[...]) for kr, a in zip(krs, a_s)]
    rk_ss = [_seg_sum(r * k2 * rk_ref[...], seg) for r, k2 in zip(rs, k2s)]
    lws = [[-DECAY_SCALE * jax.nn.sigmoid(dirs[d][0][...] + x) for x in lw_pre[d]] for d in range(2)]
    css = [[_mm01(dirs[d][2][...], lw) for lw in lws[d]] for d in range(2)]
    for i, r0 in enumerate(starts):
        rows = slice(r0, r0 + sub)
        gate_o[rows, :] = gates[i].astype(BF16)
        bonus_o[rows, :] = (rk_ss[i] * vrs[i]).astype(BF16)
        v_o[rows, :] = vrs[i].astype(BF16)
        kn = kks[i] * jnp.minimum(lax.rsqrt(kk_ss[i]), 1e12)
        b = a_s[i] * kn
        for d, (_, _, _, kn_o, r_o, b_o, k_o, et_o) in enumerate(dirs):
            cs = css[d][i]
            e_neg = jnp.exp(-cs)
            kn_o[rows, :] = (kn * jnp.exp(cs - lws[d][i])).astype(BF16)
            r_o[rows, :] = (rs[i] * jnp.exp(cs)).astype(BF16)
            b_o[rows, :] = (b * e_neg).astype(BF16)
            k_o[rows, :] = (k2s[i] * e_neg).astype(BF16)
            for c in range(sub // CHUNK):
                end = c * CHUNK + (CHUNK - 1 if d == 0 else 0)
                et_o[r0 // CHUNK + c] = jnp.exp(cs[end:end + 1, :])


def _rwkv_prep(x, prm, *, tb=1024, sub=256):
    bsz, t_len, _ = x.shape
    nt = t_len // tb
    hb = tb // 8
    n8 = t_len // 8
    n_chunk = tb // CHUNK
    cat_f, cat_b = _cum_consts(sub)
    consts = [prm["w_pb"], prm["mu_prev"], prm["mu_next"], prm["w0_f"], prm["up_f"], prm["w0_b"], prm["up_b"],
              prm["a0"], prm["a_up"], prm["g_up"], prm["k_k"], prm["k_a"], prm["r_k"], prm["seg"],
              cat_f, cat_b]
    row_spec = pl.BlockSpec((None, tb, RW_W), lambda b, t: (b, t, 0))
    rows = lambda dt: jax.ShapeDtypeStruct((bsz, t_len, RW_W), dt)
    et_spec = pl.BlockSpec((None, n_chunk, 1, RW_W), lambda b, t: (b, t, 0, 0))
    et = jax.ShapeDtypeStruct((bsz, t_len // CHUNK, 1, RW_W), F32)
    per_dir_specs = [row_spec] * 4 + [et_spec]
    per_dir_shapes = [rows(BF16)] * 4 + [et]
    return pl.pallas_call(
        functools.partial(_prep_kernel, tb=tb, nt=nt, sub=sub),
        grid=(bsz, nt),
        in_specs=[pl.BlockSpec((None, tb, D_MODEL), lambda b, t: (b, t, 0)),
                  pl.BlockSpec((None, 8, D_MODEL), lambda b, t: (b, jnp.maximum(t * hb - 1, 0), 0)),
                  pl.BlockSpec((None, 8, D_MODEL), lambda b, t: (b, jnp.minimum((t + 1) * hb, n8 - 1), 0))]
                 + [_const_spec(c.shape) for c in consts],
        out_specs=[row_spec] * 3 + per_dir_specs * 2,
        out_shape=[rows(BF16)] * 3 + per_dir_shapes * 2,
        compiler_params=_cp(2),
        name="rwkv_prep",
    )(x, x, x, *consts)


def _pair_masks(reverse):
    t = lax.broadcasted_iota(jnp.int32, (CHUNK, PAIR), 0)
    s = lax.broadcasted_iota(jnp.int32, (CHUNK, PAIR), 1) % RW_N
    incl = (s >= t) if reverse else (s <= t)
    strict = (s > t) if reverse else (s < t)
    levels = []
    m = 1
    while m < CHUNK:
        if reverse:
            levels.append(((t // m) % 2 == 0) & (s // m == t // m + 1))
        else:
            levels.append(((t // m) % 2 == 1) & (s // m == t // m - 1))
        m *= 2
    return incl, strict, levels


def _dir_where(mask_f, mask_b, x):
    h = x.shape[0] // 2
    return jnp.concatenate([jnp.where(mask_f[None], x[:h], 0.0), jnp.where(mask_b[None], x[h:], 0.0)], axis=0)


def _block_diag(x):
    rows = lax.broadcasted_iota(jnp.int32, (2 * CHUNK, PAIR), 0) // CHUNK
    lanes = lax.broadcasted_iota(jnp.int32, (2 * CHUNK, PAIR), 1) // RW_N
    x = x.astype(BF16)
    return jnp.where((rows == lanes)[None], jnp.concatenate([x, x], axis=1), jnp.zeros((), BF16))


def _chunk_operators(kn, r, b, k, v, e_tot):
    incl_f, strict_f, levels_f = _pair_masks(False)
    incl_b, strict_b, levels_b = _pair_masks(True)
    mask2_f = jnp.concatenate([strict_f, incl_f], axis=0)
    mask2_b = jnp.concatenate([strict_b, incl_b], axis=0)
    t = lax.broadcasted_iota(jnp.int32, (CHUNK, PAIR), 0)
    s = lax.broadcasted_iota(jnp.int32, (CHUNK, PAIR), 1) % RW_N
    eye = (t == s).astype(F32)
    same_head = (lax.broadcasted_iota(jnp.int32, (PAIR, PAIR), 0) // RW_N
                 == lax.broadcasted_iota(jnp.int32, (PAIR, PAIR), 1) // RW_N)[None]
    xr = jnp.concatenate([kn, r], axis=1)
    bk = jnp.concatenate([_block_diag(b), _block_diag(k)], axis=1)
    abk = _dir_where(jnp.concatenate([mask2_f, mask2_f], axis=1), jnp.concatenate([mask2_b, mask2_b], axis=1),
                     _bmm(xr, bk, 2, 2))
    ab = abk[:, :, :PAIR]
    ak = abk[:, :, PAIR:]
    av = _bmm(ak, _block_diag(v), 2, 1)
    a_b = ab[:, :CHUNK]
    a_rb = ab[:, CHUNK:]
    inv = eye[None] - _dir_where(levels_f[0], levels_b[0], a_b)
    for mk_f, mk_b in zip(levels_f[1:], levels_b[1:]):
        x = _bmm(_dir_where(mk_f, mk_b, a_b), _block_diag(inv), 2, 1)
        inv = inv - _bmm(inv, _block_diag(x), 2, 1)
    gw = _bmm(inv, jnp.concatenate([_block_diag(kn), _block_diag(av[:, :CHUNK])], axis=2), 2, 1)
    g = gw[:, :, :PAIR]
    w = gw[:, :, PAIR:]
    corr = _bmm(a_rb, jnp.concatenate([_block_diag(g), _block_diag(w)], axis=2), 2, 1)
    r_op = r.astype(F32) - corr[:, :, :PAIR]
    y0 = av[:, CHUNK:] - corr[:, :, PAIR:]
    p_op = jnp.where(same_head, _bmm(g, b, 1, 1), 0.0) * (-e_tot)
    vw = jnp.concatenate([v, w.astype(BF16)], axis=1)
    kb = jnp.concatenate([k, -b], axis=1)
    q_op = jnp.where(same_head, _bmm(vw, kb, 1, 1), 0.0) * e_tot
    return r_op, y0, p_op, q_op


def _scan_kernel(knf_ref, rf_ref, bf_ref, kf_ref, vf_ref, etf_ref,
                 knb_ref, rb_ref, bb_ref, kb_ref, vb_ref, etb_ref, yf_ref, yb_ref, s_ref, *, tb):
    @pl.when(pl.program_id(1) == 0)
    def _():
        s_ref[...] = jnp.zeros_like(s_ref)

    n_chunk = tb // CHUNK
    streams = ((knf_ref, rf_ref, bf_ref, kf_ref, vf_ref), (knb_ref, rb_ref, bb_ref, kb_ref, vb_ref))

    def stacked(ref_f, ref_b, rows):
        tiles = []
        for ref in (ref_f, ref_b):
            for c in range(n_chunk):
                x = ref[c * rows:(c + 1) * rows, :] if rows == CHUNK else ref[c]
                tiles += [x[:, p * PAIR:(p + 1) * PAIR] for p in range(N_PAIR)]
        return jnp.stack(tiles)

    ops = [stacked(streams[0][i], streams[1][i], CHUNK) for i in range(5)]
    e_tot = stacked(etf_ref, etb_ref, 1)
    r_op, y0, p_op, q_op = _chunk_operators(*ops, e_tot)

    state = s_ref[...]
    half = n_chunk * N_PAIR
    for i in range(n_chunk):
        cf, cb = i, n_chunk - 1 - i
        sel = lambda x: jnp.concatenate([x[cf * N_PAIR:(cf + 1) * N_PAIR],
                                         x[half + cb * N_PAIR:half + (cb + 1) * N_PAIR]], axis=0)
        y = _bmm(sel(r_op), state, 2, 2) + sel(y0)
        state = state * sel(e_tot) + _bmm(state, sel(p_op), 2, 1) + sel(q_op)
        for p in range(N_PAIR):
            yf_ref[cf * CHUNK:(cf + 1) * CHUNK, p * PAIR:(p + 1) * PAIR] = y[p].astype(BF16)
            yb_ref[cb * CHUNK:(cb + 1) * CHUNK, p * PAIR:(p + 1) * PAIR] = y[N_PAIR + p].astype(BF16)
    s_ref[...] = state


def _rwkv_scan(v, fwd_ops, bwd_ops, *, tb=512):
    bsz, t_len, _ = v.shape
    nt = t_len // tb
    n_chunk = tb // CHUNK

    def stream(tmap):
        rows = pl.BlockSpec((None, tb, RW_W), lambda bi, ti: (bi, tmap(ti), 0))
        et = pl.BlockSpec((None, n_chunk, 1, RW_W), lambda bi, ti: (bi, tmap(ti), 0, 0))
        return rows, [rows] * 5 + [et]

    rows_f, specs_f = stream(lambda t: t)
    rows_b, specs_b = stream(lambda t: nt - 1 - t)
    out_shape = jax.ShapeDtypeStruct(v.shape, BF16)
    kn_f, r_f, b_f, k_f, et_f = fwd_ops
    kn_b, r_b, b_b, k_b, et_b = bwd_ops
    return pl.pallas_call(
        functools.partial(_scan_kernel, tb=tb),
        grid=(bsz, nt),
        in_specs=specs_f + specs_b,
        out_specs=[rows_f, rows_b],
        out_shape=[out_shape, out_shape],
        scratch_shapes=[pltpu.VMEM((2 * N_PAIR, PAIR, PAIR), F32)],
        compiler_params=_cp(2),
        name="rwkv_scan",
    )(kn_f, r_f, b_f, k_f, v, et_f, kn_b, r_b, b_b, k_b, v, et_b)


def _mixout_kernel(of_ref, ob_ref, g_ref, gn_ref, yf_ref, yb_ref, gate_ref, bonus_ref, lg_ref, lb_ref,
                   seg_ref, x_ref, w_ref, ng_ref, nb_ref, o_ref, *, n_sub):
    seg = seg_ref[...]
    sub = x_ref.shape[0] // n_sub
    tiles = [slice(i * sub, (i + 1) * sub) for i in range(n_sub)]
    f32 = lambda ref, rows: ref[rows, :].astype(F32)
    ys = [f32(yf_ref, rows) + f32(yb_ref, rows) for rows in tiles]
    ycs = [y - _seg_sum(y, seg) * (1.0 / RW_N) for y in ys]
    variances = [_seg_sum(yc * yc, seg) * (1.0 / RW_N) for yc in ycs]
    outs = []
    for rows, yc, var in zip(tiles, ycs, variances):
        o = f32(of_ref, rows) + f32(ob_ref, rows)
        parts = []
        for h in range(GLA_HEADS):
            oh = o[:, h * GLA_DV:(h + 1) * GLA_DV]
            parts.append(oh * lax.rsqrt(jnp.mean(oh * oh, axis=-1, keepdims=True) + LN_EPS))
        g = g_ref[rows, :]
        o = jnp.concatenate(parts, axis=-1) * gn_ref[...] * (g * jax.nn.sigmoid(g))
        y = yc * lax.rsqrt(var + RW_GN_EPS) * lg_ref[...] + lb_ref[...]
        y = (y + f32(bonus_ref, rows)) * f32(gate_ref, rows)
        mixed = jnp.concatenate([o, y], axis=-1).astype(BF16)
        outs.append(jnp.dot(mixed, w_ref[...], preferred_element_type=F32))
    for rows, tm_out in zip(tiles, outs):
        o_ref[rows, :] = _layer_norm(ALPHA * x_ref[rows, :] + tm_out, ng_ref[...], nb_ref[...], LN_EPS)


def _mix_out(o_f, o_b, pa, gla_norm_g, y_f, y_b, gate, bonus, lnx_g, lnx_b, seg, x, w_out, ln_g, ln_b,
             *, tm=1024):
    bsz, t_len, _ = x.shape
    row = lambda width: pl.BlockSpec((None, tm, width), lambda b, t: (b, t, 0))
    return pl.pallas_call(
        functools.partial(_mixout_kernel, n_sub=4),
        grid=(bsz, t_len // tm),
        in_specs=[row(GLA_W), row(GLA_W),
                  pl.BlockSpec((None, tm, GLA_W), lambda b, t: (b, t, 2)),
                  _const_spec((1, GLA_W)),
                  row(RW_W), row(RW_W), row(RW_W), row(RW_W),
                  _const_spec((1, RW_W)), _const_spec((1, RW_W)), _const_spec((RW_W // 2, RW_W // 2)),
                  row(D_MODEL),
                  _const_spec((D_MODEL, D_MODEL)),
                  _const_spec((1, D_MODEL)), _const_spec((1, D_MODEL))],
        out_specs=row(D_MODEL),
        out_shape=jax.ShapeDtypeStruct(x.shape, F32),
        compiler_params=_cp(2),
        name="mix_out",
    )(o_f, o_b, pa, gla_norm_g, y_f, y_b, gate, bonus, lnx_g, lnx_b, seg, x, w_out, ln_g, ln_b)


def _memkv_kernel(m_ref, g_ref, b_ref, w_ref, k_ref, v_ref):
    m = _layer_norm(m_ref[...], g_ref[...], b_ref[...], LN_EPS)
    kv = jnp.dot(m.astype(BF16), w_ref[...], preferred_element_type=F32)
    k_ref[...] = kv[:, :D_MODEL].astype(BF16)
    v_ref[...] = kv[:, D_MODEL:].astype(BF16)


def _mem_kv(mem, g, b, w_kv):
    bsz, m_tok, _ = mem.shape
    spec = pl.BlockSpec((None, m_tok, D_MODEL), lambda i: (i, 0, 0))
    shape = jax.ShapeDtypeStruct(mem.shape, BF16)
    return pl.pallas_call(
        _memkv_kernel,
        grid=(bsz,),
        in_specs=[spec, _const_spec((1, D_MODEL)), _const_spec((1, D_MODEL)),
                  _const_spec((D_MODEL, 2 * D_MODEL))],
        out_specs=[spec, spec],
        out_shape=[shape, shape],
        compiler_params=_cp(1),
        name="mem_kv",
    )(mem, g, b, w_kv)


def _ca_kernel(x_ref, k_ref, v_ref, wq_ref, wo_ref, g_ref, b_ref, o_ref, *, n_sub):
    k = k_ref[...]
    v = v_ref[...]
    sub = x_ref.shape[0] // n_sub
    heads = [slice(h * MEM_HD, (h + 1) * MEM_HD) for h in range(MEM_HEADS)]
    xs = [x_ref[i * sub:(i + 1) * sub, :] for i in range(n_sub)]
    qs = [jnp.dot(x.astype(BF16), wq_ref[...], preferred_element_type=F32).astype(BF16) for x in xs]
    scores = [[_mm(q[:, hs], k[:, hs], _NT) * (MEM_HD ** -0.5) for hs in heads] for q in qs]
    outs = []
    for i in range(n_sub):
        parts = []
        for s, hs in zip(scores[i], heads):
            e = jnp.exp(s - jnp.max(s, axis=-1, keepdims=True))
            p = e * (1.0 / jnp.sum(e, axis=-1, keepdims=True))
            parts.append(_mm(p, v[:, hs], _NN).astype(BF16))
        outs.append(jnp.dot(jnp.concatenate(parts, axis=-1), wo_ref[...], preferred_element_type=F32))
    for i in range(n_sub):
        o_ref[i * sub:(i + 1) * sub, :] = _layer_norm(ALPHA * xs[i] + outs[i], g_ref[...], b_ref[...], LN_EPS)


def _cross_attn(x, k, v, w_q, w_o, g, b, *, tm=1024):
    bsz, t_len, _ = x.shape
    m_tok = k.shape[1]
    row = pl.BlockSpec((None, tm, D_MODEL), lambda bi, t: (bi, t, 0))
    mem = pl.BlockSpec((None, m_tok, D_MODEL), lambda bi, t: (bi, 0, 0))
    return pl.pallas_call(
        functools.partial(_ca_kernel, n_sub=4),
        grid=(bsz, t_len // tm),
        in_specs=[row, mem, mem, _const_spec((D_MODEL, D_MODEL)), _const_spec((D_MODEL, D_MODEL)),
                  _const_spec((1, D_MODEL)), _const_spec((1, D_MODEL))],
        out_specs=row,
        out_shape=jax.ShapeDtypeStruct(x.shape, F32),
        compiler_params=_cp(2),
        name="cross_attn",
    )(x, k, v, w_q, w_o, g, b)


def _pad_rows(w, start, total):
    return jnp.zeros((total, w.shape[1]), w.dtype).at[start:start + w.shape[0]].set(w)


def _layer(x, mem, p):
    bsz, t_len, _ = x.shape
    x2d = x.reshape(bsz * t_len, D_MODEL)
    x1 = _ffn_block(x2d, p["ffn1_w_in"], p["ffn1_w_out"], p["ln_ffn1_g"], p["ln_ffn1_b"])
    x1_seq = x1.reshape(bsz, t_len, D_MODEL)
    pa = _in_proj(x1, p["w_pa"]).reshape(bsz, t_len, PA_COLS)
    o_f, o_b = _gla(pa, p["gla_up_f"], p["gla_b_f"], p["gla_up_b"], p["gla_b_b"])
    v, gate, bonus, *ops = _rwkv_prep(x1_seq, p["rw"])
    y_f, y_b = _rwkv_scan(v, ops[:5], ops[5:])
    x2 = _mix_out(o_f, o_b, pa, p["gla_norm_g"], y_f, y_b, gate, bonus, p["lnx_g"], p["lnx_b"], p["rw"]["seg"],
                  x1_seq, p["w_out"], p["ln_mix_g"], p["ln_mix_b"])
    mk, mv = _mem_kv(mem, p["mem_ln_g"], p["mem_ln_b"], p["ca_w_kv"])
    x3 = _cross_attn(x2, mk, mv, p["ca_w_q"], p["ca_w_o"], p["ln_ca_g"], p["ln_ca_b"])
    x4 = _ffn_block(x3.reshape(bsz * t_len, D_MODEL), p["ffn2_w_in"], p["ffn2_w_out"],
                    p["ln_ffn2_g"], p["ln_ffn2_b"])
    return x4.reshape(bsz, t_len, D_MODEL)


def kernel(x_prompt, x_sample, mem_prompt, mem_sample, ffn1_w_in, ffn1_w_out, ln_ffn1_g, ln_ffn1_b, mix_w_in, gla_gate_up_fwd, gla_gate_b_fwd, gla_gate_up_bwd, gla_gate_b_bwd, gla_norm_g, rwkv_mu_prev, rwkv_mu_next, rwkv_w0_fwd, rwkv_w_up_fwd, rwkv_w0_bwd, rwkv_w_up_bwd, rwkv_a0, rwkv_a_up, rwkv_g_up, rwkv_k_k, rwkv_k_a, rwkv_r_k, rwkv_lnx_g, rwkv_lnx_b, mix_w_out, ln_mix_g, ln_mix_b, mem_ln_g, mem_ln_b, ca_w_q, ca_w_kv, ca_w_o, ln_ca_g, ln_ca_b, ffn2_w_in, ffn2_w_out, ln_ffn2_g, ln_ffn2_b):
    y_prompt, y_sample = x_prompt, x_sample
    for l in range(DEPTH):
        row = lambda a: a[l].reshape(1, -1)
        w_in = mix_w_in[l]
        zeros = lambda n: jnp.zeros((D_MODEL, n), F32)
        rw_off = GLA_COLS
        lr_off = rw_off + 3 * RW_W
        gd_off = lr_off + 2 * RW_DECAY_RANK + RW_AAA_RANK
        w_pa = jnp.concatenate([w_in[:, :GLA_COLS], zeros(PA_COLS - GLA_COLS)], axis=1)
        w_pb = jnp.concatenate([w_in[:, rw_off:gd_off], zeros(64), w_in[:, gd_off:]], axis=1)
        perm_mu = lambda mu: jnp.concatenate(
            [mu[:gd_off - rw_off], jnp.zeros((64,), F32), mu[gd_off - rw_off:]]).reshape(1, PB_COLS)
        rw = {
            "w_pb": w_pb.astype(BF16),
            "mu_prev": perm_mu(rwkv_mu_prev[l]), "mu_next": perm_mu(rwkv_mu_next[l]),
            "w0_f": row(rwkv_w0_fwd), "up_f": _pad_rows(rwkv_w_up_fwd[l], 0, 128).astype(BF16),
            "w0_b": row(rwkv_w0_bwd), "up_b": _pad_rows(rwkv_w_up_bwd[l], RW_DECAY_RANK, 128).astype(BF16),
            "a0": row(rwkv_a0), "a_up": _pad_rows(rwkv_a_up[l], 0, 128).astype(BF16),
            "g_up": rwkv_g_up[l].astype(BF16),
            "k_k": row(rwkv_k_k), "k_a": row(rwkv_k_a), "r_k": rwkv_r_k[l].reshape(1, RW_W),
            "seg": jnp.kron(jnp.eye(RW_HEADS // 2, dtype=F32), jnp.ones((RW_N, RW_N), F32)).astype(BF16),
        }
        p = {
            "ffn1_w_in": ffn1_w_in[l].astype(BF16), "ffn1_w_out": ffn1_w_out[l].astype(BF16),
            "ln_ffn1_g": row(ln_ffn1_g), "ln_ffn1_b": row(ln_ffn1_b),
            "w_pa": w_pa.astype(BF16),
            "gla_up_f": _pad_rows(gla_gate_up_fwd[l], 0, 128), "gla_b_f": row(gla_gate_b_fwd),
            "gla_up_b": _pad_rows(gla_gate_up_bwd[l], GLA_RANK, 128), "gla_b_b": row(gla_gate_b_bwd),
            "gla_norm_g": row(gla_norm_g),
            "rw": rw,
            "lnx_g": row(rwkv_lnx_g), "lnx_b": row(rwkv_lnx_b),
            "w_out": mix_w_out[l].astype(BF16),
            "ln_mix_g": row(ln_mix_g), "ln_mix_b": row(ln_mix_b),
            "mem_ln_g": row(mem_ln_g), "mem_ln_b": row(mem_ln_b),
            "ca_w_q": ca_w_q[l].astype(BF16), "ca_w_kv": ca_w_kv[l].astype(BF16), "ca_w_o": ca_w_o[l].astype(BF16),
            "ln_ca_g": row(ln_ca_g), "ln_ca_b": row(ln_ca_b),
            "ffn2_w_in": ffn2_w_in[l].astype(BF16), "ffn2_w_out": ffn2_w_out[l].astype(BF16),
            "ln_ffn2_g": row(ln_ffn2_g), "ln_ffn2_b": row(ln_ffn2_b),
        }
        y_prompt = _layer(y_prompt, mem_prompt, p)
        y_sample = _layer(y_sample, mem_sample, p)
    return (y_prompt, y_sample)
```

```python
import functools

import jax
import jax.numpy as jnp
from jax import lax
from jax.experimental import pallas as pl
from jax.experimental.pallas import tpu as pltpu

F32 = jnp.float32
BF16 = jnp.bfloat16

D_MODEL = 1024
D_FF = 2816
DEPTH = 1
GLA_HEADS = 4
GLA_DK = 64
GLA_DV = 128
GLA_QK = GLA_HEADS * GLA_DK
GLA_W = GLA_HEADS * GLA_DV
GLA_RANK = 16
GLA_TAU = 16.0
RW_HEADS = 8
RW_N = 64
RW_W = RW_HEADS * RW_N
RW_DECAY_RANK = 64
RW_AAA_RANK = 64
RW_GATE_RANK = 128
RW_GN_EPS = 64e-5
GLA_COLS = 1568
RW_COLS = 1856
MEM_HEADS = 4
MEM_HD = D_MODEL // MEM_HEADS
LN_EPS = 1e-5
ALPHA = (2.0 * DEPTH) ** 0.25
CHUNK = 64
PAIR = 2 * RW_N
N_PAIR = RW_HEADS // 2
DECAY_SCALE = 0.6065306597126334

PA_COLS = 1664
PB_COLS = 1920

VMEM_LIMIT = 56 * 1024 * 1024


def _cp(n_axes):
    return pltpu.CompilerParams(dimension_semantics=("arbitrary",) * n_axes,
                                vmem_limit_bytes=VMEM_LIMIT)


def _mm(a, b, dims=((1,), (0,))):
    return lax.dot_general(a.astype(BF16), b.astype(BF16), (dims, ((), ())), preferred_element_type=F32)


_NN = ((1,), (0,))
_NT = ((1,), (1,))
_TN = ((0,), (0,))


def _bmm(a, b, ca, cb):
    return lax.dot_general(a.astype(BF16), b.astype(BF16), (((ca,), (cb,)), ((0,), (0,))),
                           preferred_element_type=F32)


def _layer_norm(z, g, b, eps):
    mu = jnp.mean(z, axis=-1, keepdims=True)
    zc = z - mu
    var = jnp.mean(zc * zc, axis=-1, keepdims=True)
    return zc * lax.rsqrt(var + eps) * g + b


def _softplus(x):
    return jnp.maximum(x, 0.0) + jnp.log(1.0 + jnp.exp(-jnp.abs(x)))


def _const_spec(shape):
    nd = len(shape)
    return pl.BlockSpec(shape, lambda *_: (0,) * nd, pipeline_mode=pl.Buffered(1))


def _ffn_kernel(x_ref, win_ref, wout_ref, g_ref, b_ref, o_ref, *, n_split, n_sub):
    fc = D_FF // n_split
    sub = x_ref.shape[0] // n_sub
    for i in range(n_sub):
        rows = slice(i * sub, (i + 1) * sub)
        x = x_ref[rows, :]
        xb = x.astype(BF16)
        acc = None
        for f in range(n_split):
            gate = jnp.dot(xb, win_ref[:, f * fc:(f + 1) * fc], preferred_element_type=F32)
            up = jnp.dot(xb, win_ref[:, D_FF + f * fc:D_FF + (f + 1) * fc], preferred_element_type=F32)
            h = (gate * jax.nn.sigmoid(gate) * up).astype(BF16)
            y = jnp.dot(h, wout_ref[f * fc:(f + 1) * fc, :], preferred_element_type=F32)
            acc = y if acc is None else acc + y
        o_ref[rows, :] = _layer_norm(ALPHA * x + 0.5 * acc, g_ref[...], b_ref[...], LN_EPS)


def _ffn_block(x, w_in, w_out, g, b, *, tm=1024, n_split=11, n_sub=2):
    n = x.shape[0]
    return pl.pallas_call(
        functools.partial(_ffn_kernel, n_split=n_split, n_sub=n_sub),
        grid=(n // tm,),
        in_specs=[pl.BlockSpec((tm, D_MODEL), lambda i: (i, 0)),
                  _const_spec((D_MODEL, 2 * D_FF)),
                  _const_spec((D_FF, D_MODEL)),
                  _const_spec((1, D_MODEL)),
                  _const_spec((1, D_MODEL))],
        out_specs=pl.BlockSpec((tm, D_MODEL), lambda i: (i, 0)),
        out_shape=jax.ShapeDtypeStruct((n, D_MODEL), F32),
        compiler_params=_cp(1),
        name="ffn_block",
    )(x, w_in, w_out, g, b)


def _inproj_kernel(x_ref, wa_ref, oa_ref):
    oa_ref[...] = jnp.dot(x_ref[...].astype(BF16), wa_ref[...], preferred_element_type=F32)


def _in_proj(x, wa, *, tm=1024):
    n = x.shape[0]
    return pl.pallas_call(
        _inproj_kernel,
        grid=(n // tm,),
        in_specs=[pl.BlockSpec((tm, D_MODEL), lambda i: (i, 0)),
                  _const_spec((D_MODEL, PA_COLS))],
        out_specs=pl.BlockSpec((tm, PA_COLS), lambda i: (i, 0)),
        out_shape=jax.ShapeDtypeStruct((n, PA_COLS), F32),
        compiler_params=_cp(1),
        name="in_proj",
    )(x, wa)


def _tri_masks(reverse):
    row = lax.broadcasted_iota(jnp.int32, (CHUNK, CHUNK), 0)
    col = lax.broadcasted_iota(jnp.int32, (CHUNK, CHUNK), 1)
    incl = (col >= row) if reverse else (col <= row)
    strict = (col > row) if reverse else (col < row)
    return row, col, incl, strict


def _cum_consts(tb):
    idx = jnp.arange(tb)
    same = (idx[:, None] // CHUNK) == (idx[None, :] // CHUNK)
    fwd = same & (idx[None, :] <= idx[:, None])
    bwd = same & (idx[None, :] >= idx[:, None])
    return fwd.astype(BF16), bwd.astype(BF16)


def _mm01(m01, x):
    hi = x.astype(BF16)
    r1 = x - hi.astype(F32)
    mid = r1.astype(BF16)
    lo = (r1 - mid.astype(F32)).astype(BF16)
    dot = lambda p: jnp.dot(m01, p, preferred_element_type=F32)
    return dot(hi) + dot(mid) + dot(lo)


def _gla_kernel(qf_ref, kf_ref, vf_ref, gf_ref, qb_ref, kb_ref, vb_ref, gb_ref,
                upf_ref, bf_ref, upb_ref, bb_ref, cf_ref, cb_ref, of_ref, ob_ref, s_ref, *, tb):
    @pl.when(pl.program_id(1) == 0)
    def _():
        s_ref[...] = jnp.zeros_like(s_ref)

    n_chunk = tb // CHUNK
    n_pair = GLA_HEADS // 2
    pv = 2 * GLA_DV
    streams = ((qf_ref, kf_ref, vf_ref, gf_ref, upf_ref, bf_ref, cf_ref),
               (qb_ref, kb_ref, vb_ref, gb_ref, upb_ref, bb_ref, cb_ref))
    qt_t, kt_t, qs_t, ks_t, v_t, dec_t = [], [], [], [], [], []
    for d, (q_ref, k_ref, v_ref, g_ref, up_ref, gbias_ref, tri_ref) in enumerate(streams):
        z = _mm(g_ref[...], up_ref[...]) + gbias_ref[...]
        log_a = -_softplus(-z) / GLA_TAU
        b = _mm01(tri_ref[...], log_a)
        q = q_ref[...] * (GLA_DK ** -0.5)
        k = k_ref[...]
        v = v_ref[...].astype(BF16)
        for c in range(n_chunk):
            rows = slice(c * CHUNK, (c + 1) * CHUNK)
            end = c * CHUNK + (CHUNK - 1 if d == 0 else 0)
            half = 0.5 * b[end:end + 1, :]
            e_half = jnp.exp(half)
            qt = q[rows, :] * jnp.exp(b[rows, :] - half)
            kt = k[rows, :] * jnp.exp(half - b[rows, :])
            for p in range(n_pair):
                lanes = slice(p * PAIR, (p + 1) * PAIR)
                qt_t.append(qt[:, lanes])
                kt_t.append(kt[:, lanes])
                qs_t.append(qt[:, lanes] * e_half[:, lanes])
                ks_t.append(kt[:, lanes] * e_half[:, lanes])
                dec_t.append(e_half[:, lanes] * e_half[:, lanes])
                v_t.append(v[rows, p * pv:(p + 1) * pv])
    qt_s, kt_s, qs_s, ks_s, v_s = (jnp.stack(x) for x in (qt_t, kt_t, qs_t, ks_t, v_t))

    incl_f = _pair_masks(False)[0]
    incl_b = _pair_masks(True)[0]
    scores = _dir_where(incl_f, incl_b, _bmm(qt_s, _block_diag(kt_s), 2, 2))
    v_rows = lax.broadcasted_iota(jnp.int32, (PAIR, pv), 0) // CHUNK
    v_lanes = lax.broadcasted_iota(jnp.int32, (PAIR, pv), 1) // GLA_DV
    v_bd = jnp.where((v_rows == v_lanes)[None], jnp.concatenate([v_s, v_s], axis=1), jnp.zeros((), BF16))
    intra = _bmm(scores, v_bd, 2, 1)
    same_head = (lax.broadcasted_iota(jnp.int32, (pv, PAIR), 0) // GLA_DV
                 == lax.broadcasted_iota(jnp.int32, (pv, PAIR), 1) // GLA_DK)
    kv = jnp.where(same_head[None], _bmm(v_s, ks_s, 1, 1), 0.0)

    idx = lambda d, c, p: (d * n_chunk + c) * n_pair + p
    s_prev = [None] * (2 * n_chunk * n_pair)
    for d in range(2):
        for p in range(n_pair):
            state = s_ref[d * n_pair + p]
            for i in range(n_chunk):
                c = i if d == 0 else n_chunk - 1 - i
                s_prev[idx(d, c, p)] = state
                state = state * dec_t[idx(d, c, p)] + kv[idx(d, c, p)]
            s_ref[d * n_pair + p] = state
    out = intra + _bmm(qs_s, jnp.stack(s_prev), 2, 2)
    for d, o_ref in enumerate((of_ref, ob_ref)):
        for c in range(n_chunk):
            for p in range(n_pair):
                o_ref[c * CHUNK:(c + 1) * CHUNK, p * pv:(p + 1) * pv] = out[idx(d, c, p)].astype(BF16)


def _gla(pa, up_f, bias_f, up_b, bias_b, *, tb=256):
    bsz, t_len, _ = pa.shape
    nt = t_len // tb
    cat_f, cat_b = _cum_consts(tb)

    def stream(tmap):
        return [pl.BlockSpec((None, tb, GLA_QK), lambda b, t: (b, tmap(t), 0)),
                pl.BlockSpec((None, tb, GLA_QK), lambda b, t: (b, tmap(t), 1)),
                pl.BlockSpec((None, tb, GLA_W), lambda b, t: (b, tmap(t), 1)),
                pl.BlockSpec((None, tb, 128), lambda b, t: (b, tmap(t), 12))]

    fwd = lambda t: t
    bwd = lambda t: nt - 1 - t
    out_shape = jax.ShapeDtypeStruct((bsz, t_len, GLA_W), BF16)
    return pl.pallas_call(
        functools.partial(_gla_kernel, tb=tb),
        grid=(bsz, nt),
        in_specs=stream(fwd) + stream(bwd)
                 + [_const_spec((128, GLA_QK)), _const_spec((1, GLA_QK)),
                    _const_spec((128, GLA_QK)), _const_spec((1, GLA_QK)),
                    _const_spec((tb, tb)), _const_spec((tb, tb))],
        out_specs=[pl.BlockSpec((None, tb, GLA_W), lambda b, t: (b, fwd(t), 0)),
                   pl.BlockSpec((None, tb, GLA_W), lambda b, t: (b, bwd(t), 0))],
        out_shape=[out_shape, out_shape],
        scratch_shapes=[pltpu.VMEM((GLA_HEADS, 2 * GLA_DV, 2 * GLA_DK), F32)],
        compiler_params=_cp(2),
        name="gla",
    )(pa, pa, pa, pa, pa, pa, pa, pa, up_f, bias_f, up_b, bias_b, cat_f, cat_b)


def _seg_sum(x, seg):
    hi = x.astype(BF16)
    lo = (x - hi.astype(F32)).astype(BF16)
    w = seg.shape[0]
    halves = [jnp.dot(hi[:, c:c + w], seg, preferred_element_type=F32)
              + jnp.dot(lo[:, c:c + w], seg, preferred_element_type=F32) for c in range(0, x.shape[1], w)]
    return jnp.concatenate(halves, axis=1)


def _prep_kernel(x_ref, xp_ref, xn_ref, wpb_ref, mup_ref, mun_ref, w0f_ref, upf_ref, w0b_ref, upb_ref,
                 a0_ref, aup_ref, gup_ref, kk_ref, ka_ref, rk_ref, seg_ref, cf_ref, cb_ref,
                 v_o, gate_o, bonus_o, knf_o, rf_o, bf_o, kf_o, etf_o, knb_o, rb_o, bb_o, kb_o, etb_o,
                 *, tb, nt, sub):
    t = pl.program_id(1)
    halo_prev = jnp.where(t > 0, xp_ref[...], 0.0)
    halo_next = jnp.where(t < nt - 1, xn_ref[...], 0.0)
    x_ext = jnp.concatenate([halo_prev, x_ref[...], halo_next], axis=0).astype(BF16)
    starts = range(0, tb, sub)
    p_exts = [jnp.dot(x_ext[r0:r0 + sub + 16], wpb_ref[...], preferred_element_type=F32) for r0 in starts]
    seg = seg_ref[...]
    dirs = ((w0f_ref, upf_ref, cf_ref, knf_o, rf_o, bf_o, kf_o, etf_o),
            (w0b_ref, upb_ref, cb_ref, knb_o, rb_o, bb_o, kb_o, etb_o))
    rws = []
    for p_ext in p_exts:
        p = p_ext[8:8 + sub]
        rws.append(p + mup_ref[...] * (p_ext[7:7 + sub] - p) + mun_ref[...] * (p_ext[9:9 + sub] - p))
    rs = [rw[:, 0:512] for rw in rws]
    krs = [rw[:, 512:1024] for rw in rws]
    vrs = [rw[:, 1024:1536] for rw in rws]
    wds = [jnp.tanh(rw[:, 1536:1664]) for rw in rws]
    a_pre = [_mm(rw[:, 1664:1792], aup_ref[...]) for rw in rws]
    gates = [_mm(jax.nn.sigmoid(rw[:, 1792:1920]), gup_ref[...]) for rw in rws]
    lw_pre = [[_mm(wd, dirs[d][1][...]) for wd in wds] for d in range(2)]
    kks = [kr * kk_ref[...] for kr in krs]
    kk_ss = [_seg_sum(kk * kk, seg) for kk in kks]
    a_s = [jax.nn.sigmoid(a0_ref[...] + ap) for ap in a_pre]
    k2s = [kr * (1.0 + (a - 1.0) * ka_ref[...]) for kr, a in zip(krs, a_s)]
    rk_ss = [_seg_sum(r * k2 * rk_ref[...], seg) for r, k2 in zip(rs, k2s)]
    lws = [[-DECAY_SCALE * jax.nn.sigmoid(dirs[d][0][...] + x) for x in lw_pre[d]] for d in range(2)]
    css = [[_mm01(dirs[d][2][...], lw) for lw in lws[d]] for d in range(2)]
    for i, r0 in enumerate(starts):
        rows = slice(r0, r0 + sub)
        gate_o[rows, :] = gates[i].astype(BF16)
        bonus_o[rows, :] = (rk_ss[i] * vrs[i]).astype(BF16)
        v_o[rows, :] = vrs[i].astype(BF16)
        kn = kks[i] * jnp.minimum(lax.rsqrt(kk_ss[i]), 1e12)
        b = a_s[i] * kn
        for d, (_, _, _, kn_o, r_o, b_o, k_o, et_o) in enumerate(dirs):
            cs = css[d][i]
            e_neg = jnp.exp(-cs)
            kn_o[rows, :] = (kn * jnp.exp(cs - lws[d][i])).astype(BF16)
            r_o[rows, :] = (rs[i] * jnp.exp(cs)).astype(BF16)
            b_o[rows, :] = (b * e_neg).astype(BF16)
            k_o[rows, :] = (k2s[i] * e_neg).astype(BF16)
            for c in range(sub // CHUNK):
                end = c * CHUNK + (CHUNK - 1 if d == 0 else 0)
                et_o[r0 // CHUNK + c] = jnp.exp(cs[end:end + 1, :])


def _rwkv_prep(x, prm, *, tb=1024, sub=256):
    bsz, t_len, _ = x.shape
    nt = t_len // tb
    hb = tb // 8
    n8 = t_len // 8
    n_chunk = tb // CHUNK
    cat_f, cat_b = _cum_consts(sub)
    consts = [prm["w_pb"], prm["mu_prev"], prm["mu_next"], prm["w0_f"], prm["up_f"], prm["w0_b"], prm["up_b"],
              prm["a0"], prm["a_up"], prm["g_up"], prm["k_k"], prm["k_a"], prm["r_k"], prm["seg"],
              cat_f, cat_b]
    row_spec = pl.BlockSpec((None, tb, RW_W), lambda b, t: (b, t, 0))
    rows = lambda dt: jax.ShapeDtypeStruct((bsz, t_len, RW_W), dt)
    et_spec = pl.BlockSpec((None, n_chunk, 1, RW_W), lambda b, t: (b, t, 0, 0))
    et = jax.ShapeDtypeStruct((bsz, t_len // CHUNK, 1, RW_W), F32)
    per_dir_specs = [row_spec] * 4 + [et_spec]
    per_dir_shapes = [rows(BF16)] * 4 + [et]
    return pl.pallas_call(
        functools.partial(_prep_kernel, tb=tb, nt=nt, sub=sub),
        grid=(bsz, nt),
        in_specs=[pl.BlockSpec((None, tb, D_MODEL), lambda b, t: (b, t, 0)),
                  pl.BlockSpec((None, 8, D_MODEL), lambda b, t: (b, jnp.maximum(t * hb - 1, 0), 0)),
                  pl.BlockSpec((None, 8, D_MODEL), lambda b, t: (b, jnp.minimum((t + 1) * hb, n8 - 1), 0))]
                 + [_const_spec(c.shape) for c in consts],
        out_specs=[row_spec] * 3 + per_dir_specs * 2,
        out_shape=[rows(BF16)] * 3 + per_dir_shapes * 2,
        compiler_params=_cp(2),
        name="rwkv_prep",
    )(x, x, x, *consts)


def _pair_masks(reverse):
    t = lax.broadcasted_iota(jnp.int32, (CHUNK, PAIR), 0)
    s = lax.broadcasted_iota(jnp.int32, (CHUNK, PAIR), 1) % RW_N
    incl = (s >= t) if reverse else (s <= t)
    strict = (s > t) if reverse else (s < t)
    levels = []
    m = 1
    while m < CHUNK:
        if reverse:
            levels.append(((t // m) % 2 == 0) & (s // m == t // m + 1))
        else:
            levels.append(((t // m) % 2 == 1) & (s // m == t // m - 1))
        m *= 2
    return incl, strict, levels


def _dir_where(mask_f, mask_b, x):
    h = x.shape[0] // 2
    return jnp.concatenate([jnp.where(mask_f[None], x[:h], 0.0), jnp.where(mask_b[None], x[h:], 0.0)], axis=0)


def _block_diag(x):
    rows = lax.broadcasted_iota(jnp.int32, (2 * CHUNK, PAIR), 0) // CHUNK
    lanes = lax.broadcasted_iota(jnp.int32, (2 * CHUNK, PAIR), 1) // RW_N
    x = x.astype(BF16)
    return jnp.where((rows == lanes)[None], jnp.concatenate([x, x], axis=1), jnp.zeros((), BF16))


def _chunk_operators(kn, r, b, k, v, e_tot):
    incl_f, strict_f, levels_f = _pair_masks(False)
    incl_b, strict_b, levels_b = _pair_masks(True)
    mask2_f = jnp.concatenate([strict_f, incl_f], axis=0)
    mask2_b = jnp.concatenate([strict_b, incl_b], axis=0)
    t = lax.broadcasted_iota(jnp.int32, (CHUNK, PAIR), 0)
    s = lax.broadcasted_iota(jnp.int32, (CHUNK, PAIR), 1) % RW_N
    eye = (t == s).astype(F32)
    same_head = (lax.broadcasted_iota(jnp.int32, (PAIR, PAIR), 0) // RW_N
                 == lax.broadcasted_iota(jnp.int32, (PAIR, PAIR), 1) // RW_N)[None]
    xr = jnp.concatenate([kn, r], axis=1)
    bk = jnp.concatenate([_block_diag(b), _block_diag(k)], axis=1)
    abk = _dir_where(jnp.concatenate([mask2_f, mask2_f], axis=1), jnp.concatenate([mask2_b, mask2_b], axis=1),
                     _bmm(xr, bk, 2, 2))
    ab = abk[:, :, :PAIR]
    ak = abk[:, :, PAIR:]
    av = _bmm(ak, _block_diag(v), 2, 1)
    a_b = ab[:, :CHUNK]
    a_rb = ab[:, CHUNK:]
    inv = eye[None] - _dir_where(levels_f[0], levels_b[0], a_b)
    for mk_f, mk_b in zip(levels_f[1:], levels_b[1:]):
        x = _bmm(_dir_where(mk_f, mk_b, a_b), _block_diag(inv), 2, 1)
        inv = inv - _bmm(inv, _block_diag(x), 2, 1)
    gw = _bmm(inv, jnp.concatenate([_block_diag(kn), _block_diag(av[:, :CHUNK])], axis=2), 2, 1)
    g = gw[:, :, :PAIR]
    w = gw[:, :, PAIR:]
    corr = _bmm(a_rb, jnp.concatenate([_block_diag(g), _block_diag(w)], axis=2), 2, 1)
    r_op = r.astype(F32) - corr[:, :, :PAIR]
    y0 = av[:, CHUNK:] - corr[:, :, PAIR:]
    p_op = jnp.where(same_head, _bmm(g, b, 1, 1), 0.0) * (-e_tot)
    vw = jnp.concatenate([v, w.astype(BF16)], axis=1)
    kb = jnp.concatenate([k, -b], axis=1)
    q_op = jnp.where(same_head, _bmm(vw, kb, 1, 1), 0.0) * e_tot
    return r_op, y0, p_op, q_op


def _scan_kernel(knf_ref, rf_ref, bf_ref, kf_ref, vf_ref, etf_ref,
                 knb_ref, rb_ref, bb_ref, kb_ref, vb_ref, etb_ref, yf_ref, yb_ref, s_ref, *, tb):
    @pl.when(pl.program_id(1) == 0)
    def _():
        s_ref[...] = jnp.zeros_like(s_ref)

    n_chunk = tb // CHUNK
    streams = ((knf_ref, rf_ref, bf_ref, kf_ref, vf_ref), (knb_ref, rb_ref, bb_ref, kb_ref, vb_ref))

    def stacked(ref_f, ref_b, rows):
        tiles = []
        for ref in (ref_f, ref_b):
            for c in range(n_chunk):
                x = ref[c * rows:(c + 1) * rows, :] if rows == CHUNK else ref[c]
                tiles += [x[:, p * PAIR:(p + 1) * PAIR] for p in range(N_PAIR)]
        return jnp.stack(tiles)

    ops = [stacked(streams[0][i], streams[1][i], CHUNK) for i in range(5)]
    e_tot = stacked(etf_ref, etb_ref, 1)
    r_op, y0, p_op, q_op = _chunk_operators(*ops, e_tot)

    state = s_ref[...]
    half = n_chunk * N_PAIR
    for i in range(n_chunk):
        cf, cb = i, n_chunk - 1 - i
        sel = lambda x: jnp.concatenate([x[cf * N_PAIR:(cf + 1) * N_PAIR],
                                         x[half + cb * N_PAIR:half + (cb + 1) * N_PAIR]], axis=0)
        y = _bmm(sel(r_op), state, 2, 2) + sel(y0)
        state = state * sel(e_tot) + _bmm(state, sel(p_op), 2, 1) + sel(q_op)
        for p in range(N_PAIR):
            yf_ref[cf * CHUNK:(cf + 1) * CHUNK, p * PAIR:(p + 1) * PAIR] = y[p].astype(BF16)
            yb_ref[cb * CHUNK:(cb + 1) * CHUNK, p * PAIR:(p + 1) * PAIR] = y[N_PAIR + p].astype(BF16)
    s_ref[...] = state


def _rwkv_scan(v, fwd_ops, bwd_ops, *, tb=512):
    bsz, t_len, _ = v.shape
    nt = t_len // tb
    n_chunk = tb // CHUNK

    def stream(tmap):
        rows = pl.BlockSpec((None, tb, RW_W), lambda bi, ti: (bi, tmap(ti), 0))
        et = pl.BlockSpec((None, n_chunk, 1, RW_W), lambda bi, ti: (bi, tmap(ti), 0, 0))
        return rows, [rows] * 5 + [et]

    rows_f, specs_f = stream(lambda t: t)
    rows_b, specs_b = stream(lambda t: nt - 1 - t)
    out_shape = jax.ShapeDtypeStruct(v.shape, BF16)
    kn_f, r_f, b_f, k_f, et_f = fwd_ops
    kn_b, r_b, b_b, k_b, et_b = bwd_ops
    return pl.pallas_call(
        functools.partial(_scan_kernel, tb=tb),
        grid=(bsz, nt),
        in_specs=specs_f + specs_b,
        out_specs=[rows_f, rows_b],
        out_shape=[out_shape, out_shape],
        scratch_shapes=[pltpu.VMEM((2 * N_PAIR, PAIR, PAIR), F32)],
        compiler_params=_cp(2),
        name="rwkv_scan",
    )(kn_f, r_f, b_f, k_f, v, et_f, kn_b, r_b, b_b, k_b, v, et_b)


def _mixout_kernel(of_ref, ob_ref, g_ref, gn_ref, yf_ref, yb_ref, gate_ref, bonus_ref, lg_ref, lb_ref,
                   seg_ref, x_ref, w_ref, ng_ref, nb_ref, o_ref, *, n_sub):
    seg = seg_ref[...]
    sub = x_ref.shape[0] // n_sub
    tiles = [slice(i * sub, (i + 1) * sub) for i in range(n_sub)]
    f32 = lambda ref, rows: ref[rows, :].astype(F32)
    ys = [f32(yf_ref, rows) + f32(yb_ref, rows) for rows in tiles]
    ycs = [y - _seg_sum(y, seg) * (1.0 / RW_N) for y in ys]
    variances = [_seg_sum(yc * yc, seg) * (1.0 / RW_N) for yc in ycs]
    outs = []
    for rows, yc, var in zip(tiles, ycs, variances):
        o = f32(of_ref, rows) + f32(ob_ref, rows)
        parts = []
        for h in range(GLA_HEADS):
            oh = o[:, h * GLA_DV:(h + 1) * GLA_DV]
            parts.append(oh * lax.rsqrt(jnp.mean(oh * oh, axis=-1, keepdims=True) + LN_EPS))
        g = g_ref[rows, :]
        o = jnp.concatenate(parts, axis=-1) * gn_ref[...] * (g * jax.nn.sigmoid(g))
        y = yc * lax.rsqrt(var + RW_GN_EPS) * lg_ref[...] + lb_ref[...]
        y = (y + f32(bonus_ref, rows)) * f32(gate_ref, rows)
        mixed = jnp.concatenate([o, y], axis=-1).astype(BF16)
        outs.append(jnp.dot(mixed, w_ref[...], preferred_element_type=F32))
    for rows, tm_out in zip(tiles, outs):
        o_ref[rows, :] = _layer_norm(ALPHA * x_ref[rows, :] + tm_out, ng_ref[...], nb_ref[...], LN_EPS)


def _mix_out(o_f, o_b, pa, gla_norm_g, y_f, y_b, gate, bonus, lnx_g, lnx_b, seg, x, w_out, ln_g, ln_b,
             *, tm=1024):
    bsz, t_len, _ = x.shape
    row = lambda width: pl.BlockSpec((None, tm, width), lambda b, t: (b, t, 0))
    return pl.pallas_call(
        functools.partial(_mixout_kernel, n_sub=4),
        grid=(bsz, t_len // tm),
        in_specs=[row(GLA_W), row(GLA_W),
                  pl.BlockSpec((None, tm, GLA_W), lambda b, t: (b, t, 2)),
                  _const_spec((1, GLA_W)),
                  row(RW_W), row(RW_W), row(RW_W), row(RW_W),
                  _const_spec((1, RW_W)), _const_spec((1, RW_W)), _const_spec((RW_W // 2, RW_W // 2)),
                  row(D_MODEL),
                  _const_spec((D_MODEL, D_MODEL)),
                  _const_spec((1, D_MODEL)), _const_spec((1, D_MODEL))],
        out_specs=row(D_MODEL),
        out_shape=jax.ShapeDtypeStruct(x.shape, F32),
        compiler_params=_cp(2),
        name="mix_out",
    )(o_f, o_b, pa, gla_norm_g, y_f, y_b, gate, bonus, lnx_g, lnx_b, seg, x, w_out, ln_g, ln_b)


def _memkv_kernel(m_ref, g_ref, b_ref, w_ref, k_ref, v_ref):
    m = _layer_norm(m_ref[...], g_ref[...], b_ref[...], LN_EPS)
    kv = jnp.dot(m.astype(BF16), w_ref[...], preferred_element_type=F32)
    k_ref[...] = kv[:, :D_MODEL].astype(BF16)
    v_ref[...] = kv[:, D_MODEL:].astype(BF16)


def _mem_kv(mem, g, b, w_kv):
    bsz, m_tok, _ = mem.shape
    spec = pl.BlockSpec((None, m_tok, D_MODEL), lambda i: (i, 0, 0))
    shape = jax.ShapeDtypeStruct(mem.shape, BF16)
    return pl.pallas_call(
        _memkv_kernel,
        grid=(bsz,),
        in_specs=[spec, _const_spec((1, D_MODEL)), _const_spec((1, D_MODEL)),
                  _const_spec((D_MODEL, 2 * D_MODEL))],
        out_specs=[spec, spec],
        out_shape=[shape, shape],
        compiler_params=_cp(1),
        name="mem_kv",
    )(mem, g, b, w_kv)


def _ca_kernel(x_ref, k_ref, v_ref, wq_ref, wo_ref, g_ref, b_ref, o_ref, *, n_sub):
    k = k_ref[...]
    v = v_ref[...]
    sub = x_ref.shape[0] // n_sub
    heads = [slice(h * MEM_HD, (h + 1) * MEM_HD) for h in range(MEM_HEADS)]
    xs = [x_ref[i * sub:(i + 1) * sub, :] for i in range(n_sub)]
    qs = [jnp.dot(x.astype(BF16), wq_ref[...], preferred_element_type=F32).astype(BF16) for x in xs]
    scores = [[_mm(q[:, hs], k[:, hs], _NT) * (MEM_HD ** -0.5) for hs in heads] for q in qs]
    outs = []
    for i in range(n_sub):
        parts = []
        for s, hs in zip(scores[i], heads):
            e = jnp.exp(s - jnp.max(s, axis=-1, keepdims=True))
            p = e * (1.0 / jnp.sum(e, axis=-1, keepdims=True))
            parts.append(_mm(p, v[:, hs], _NN).astype(BF16))
        outs.append(jnp.dot(jnp.concatenate(parts, axis=-1), wo_ref[...], preferred_element_type=F32))
    for i in range(n_sub):
        o_ref[i * sub:(i + 1) * sub, :] = _layer_norm(ALPHA * xs[i] + outs[i], g_ref[...], b_ref[...], LN_EPS)


def _cross_attn(x, k, v, w_q, w_o, g, b, *, tm=1024):
    bsz, t_len, _ = x.shape
    m_tok = k.shape[1]
    row = pl.BlockSpec((None, tm, D_MODEL), lambda bi, t: (bi, t, 0))
    mem = pl.BlockSpec((None, m_tok, D_MODEL), lambda bi, t: (bi, 0, 0))
    return pl.pallas_call(
        functools.partial(_ca_kernel, n_sub=4),
        grid=(bsz, t_len // tm),
        in_specs=[row, mem, mem, _const_spec((D_MODEL, D_MODEL)), _const_spec((D_MODEL, D_MODEL)),
                  _const_spec((1, D_MODEL)), _const_spec((1, D_MODEL))],
        out_specs=row,
        out_shape=jax.ShapeDtypeStruct(x.shape, F32),
        compiler_params=_cp(2),
        name="cross_attn",
    )(x, k, v, w_q, w_o, g, b)


def _pad_rows(w, start, total):
    return jnp.zeros((total, w.shape[1]), w.dtype).at[start:start + w.shape[0]].set(w)


def _layer(x, mem, p):
    bsz, t_len, _ = x.shape
    x2d = x.reshape(bsz * t_len, D_MODEL)
    x1 = _ffn_block(x2d, p["ffn1_w_in"], p["ffn1_w_out"], p["ln_ffn1_g"], p["ln_ffn1_b"])
    x1_seq = x1.reshape(bsz, t_len, D_MODEL)
    pa = _in_proj(x1, p["w_pa"]).reshape(bsz, t_len, PA_COLS)
    o_f, o_b = _gla(pa, p["gla_up_f"], p["gla_b_f"], p["gla_up_b"], p["gla_b_b"])
    v, gate, bonus, *ops = _rwkv_prep(x1_seq, p["rw"])
    y_f, y_b = _rwkv_scan(v, ops[:5], ops[5:])
    x2 = _mix_out(o_f, o_b, pa, p["gla_norm_g"], y_f, y_b, gate, bonus, p["lnx_g"], p["lnx_b"], p["rw"]["seg"],
                  x1_seq, p["w_out"], p["ln_mix_g"], p["ln_mix_b"])
    mk, mv = _mem_kv(mem, p["mem_ln_g"], p["mem_ln_b"], p["ca_w_kv"])
    x3 = _cross_attn(x2, mk, mv, p["ca_w_q"], p["ca_w_o"], p["ln_ca_g"], p["ln_ca_b"])
    x4 = _ffn_block(x3.reshape(bsz * t_len, D_MODEL), p["ffn2_w_in"], p["ffn2_w_out"],
                    p["ln_ffn2_g"], p["ln_ffn2_b"])
    return x4.reshape(bsz, t_len, D_MODEL)


def kernel(x_prompt, x_sample, mem_prompt, mem_sample, ffn1_w_in, ffn1_w_out, ln_ffn1_g, ln_ffn1_b, mix_w_in, gla_gate_up_fwd, gla_gate_b_fwd, gla_gate_up_bwd, gla_gate_b_bwd, gla_norm_g, rwkv_mu_prev, rwkv_mu_next, rwkv_w0_fwd, rwkv_w_up_fwd, rwkv_w0_bwd, rwkv_w_up_bwd, rwkv_a0, rwkv_a_up, rwkv_g_up, rwkv_k_k, rwkv_k_a, rwkv_r_k, rwkv_lnx_g, rwkv_lnx_b, mix_w_out, ln_mix_g, ln_mix_b, mem_ln_g, mem_ln_b, ca_w_q, ca_w_kv, ca_w_o, ln_ca_g, ln_ca_b, ffn2_w_in, ffn2_w_out, ln_ffn2_g, ln_ffn2_b):
    y_prompt, y_sample = x_prompt, x_sample
    for l in range(DEPTH):
        row = lambda a: a[l].reshape(1, -1)
        w_in = mix_w_in[l]
        zeros = lambda n: jnp.zeros((D_MODEL, n), F32)
        rw_off = GLA_COLS
        lr_off = rw_off + 3 * RW_W
        gd_off = lr_off + 2 * RW_DECAY_RANK + RW_AAA_RANK
        w_pa = jnp.concatenate([w_in[:, :GLA_COLS], zeros(PA_COLS - GLA_COLS)], axis=1)
        w_pb = jnp.concatenate([w_in[:, rw_off:gd_off], zeros(64), w_in[:, gd_off:]], axis=1)
        perm_mu = lambda mu: jnp.concatenate(
            [mu[:gd_off - rw_off], jnp.zeros((64,), F32), mu[gd_off - rw_off:]]).reshape(1, PB_COLS)
        rw = {
            "w_pb": w_pb.astype(BF16),
            "mu_prev": perm_mu(rwkv_mu_prev[l]), "mu_next": perm_mu(rwkv_mu_next[l]),
            "w0_f": row(rwkv_w0_fwd), "up_f": _pad_rows(rwkv_w_up_fwd[l], 0, 128).astype(BF16),
            "w0_b": row(rwkv_w0_bwd), "up_b": _pad_rows(rwkv_w_up_bwd[l], RW_DECAY_RANK, 128).astype(BF16),
            "a0": row(rwkv_a0), "a_up": _pad_rows(rwkv_a_up[l], 0, 128).astype(BF16),
            "g_up": rwkv_g_up[l].astype(BF16),
            "k_k": row(rwkv_k_k), "k_a": row(rwkv_k_a), "r_k": rwkv_r_k[l].reshape(1, RW_W),
            "seg": jnp.kron(jnp.eye(RW_HEADS // 2, dtype=F32), jnp.ones((RW_N, RW_N), F32)).astype(BF16),
        }
        p = {
            "ffn1_w_in": ffn1_w_in[l].astype(BF16), "ffn1_w_out": ffn1_w_out[l].astype(BF16),
            "ln_ffn1_g": row(ln_ffn1_g), "ln_ffn1_b": row(ln_ffn1_b),
            "w_pa": w_pa.astype(BF16),
            "gla_up_f": _pad_rows(gla_gate_up_fwd[l], 0, 128), "gla_b_f": row(gla_gate_b_fwd),
            "gla_up_b": _pad_rows(gla_gate_up_bwd[l], GLA_RANK, 128), "gla_b_b": row(gla_gate_b_bwd),
            "gla_norm_g": row(gla_norm_g),
            "rw": rw,
            "lnx_g": row(rwkv_lnx_g), "lnx_b": row(rwkv_lnx_b),
            "w_out": mix_w_out[l].astype(BF16),
            "ln_mix_g": row(ln_mix_g), "ln_mix_b": row(ln_mix_b),
            "mem_ln_g": row(mem_ln_g), "mem_ln_b": row(mem_ln_b),
            "ca_w_q": ca_w_q[l].astype(BF16), "ca_w_kv": ca_w_kv[l].astype(BF16), "ca_w_o": ca_w_o[l].astype(BF16),
            "ln_ca_g": row(ln_ca_g), "ln_ca_b": row(ln_ca_b),
            "ffn2_w_in": ffn2_w_in[l].astype(BF16), "ffn2_w_out": ffn2_w_out[l].astype(BF16),
            "ln_ffn2_g": row(ln_ffn2_g), "ln_ffn2_b": row(ln_ffn2_b),
        }
        y_prompt = _layer(y_prompt, mem_prompt, p)
        y_sample = _layer(y_sample, mem_sample, p)
    return (y_prompt, y_sample)
```

```python
import functools

import jax
import jax.numpy as jnp
from jax import lax
from jax.experimental import pallas as pl
from jax.experimental.pallas import tpu as pltpu

F32 = jnp.float32
BF16 = jnp.bfloat16

D_MODEL = 1024
D_FF = 2816
DEPTH = 1
GLA_HEADS = 4
GLA_DK = 64
GLA_DV = 128
GLA_QK = GLA_HEADS * GLA_DK
GLA_W = GLA_HEADS * GLA_DV
GLA_RANK = 16
GLA_TAU = 16.0
RW_HEADS = 8
RW_N = 64
RW_W = RW_HEADS * RW_N
RW_DECAY_RANK = 64
RW_AAA_RANK = 64
RW_GATE_RANK = 128
RW_GN_EPS = 64e-5
GLA_COLS = 1568
RW_COLS = 1856
MEM_HEADS = 4
MEM_HD = D_MODEL // MEM_HEADS
LN_EPS = 1e-5
ALPHA = (2.0 * DEPTH) ** 0.25
CHUNK = 64
PAIR = 2 * RW_N
N_PAIR = RW_HEADS // 2
DECAY_SCALE = 0.6065306597126334

PA_COLS = 1536
RW_P_COLS = 1920
PB_COLS = 2048

VMEM_LIMIT = 56 * 1024 * 1024


def _cp(n_axes):
    return pltpu.CompilerParams(dimension_semantics=("arbitrary",) * n_axes,
                                vmem_limit_bytes=VMEM_LIMIT)


def _mm(a, b, dims=((1,), (0,))):
    return lax.dot_general(a.astype(BF16), b.astype(BF16), (dims, ((), ())), preferred_element_type=F32)


_NN = ((1,), (0,))
_NT = ((1,), (1,))
_TN = ((0,), (0,))


def _bmm(a, b, ca, cb):
    return lax.dot_general(a.astype(BF16), b.astype(BF16), (((ca,), (cb,)), ((0,), (0,))),
                           preferred_element_type=F32)


def _layer_norm(z, g, b, eps):
    mu = jnp.mean(z, axis=-1, keepdims=True)
    zc = z - mu
    var = jnp.mean(zc * zc, axis=-1, keepdims=True)
    return zc * lax.rsqrt(var + eps) * g + b


def _softplus(x):
    return jnp.maximum(x, 0.0) + jnp.log(1.0 + jnp.exp(-jnp.abs(x)))


def _const_spec(shape):
    nd = len(shape)
    return pl.BlockSpec(shape, lambda *_: (0,) * nd, pipeline_mode=pl.Buffered(1))


def _ffn_kernel(x_ref, win_ref, wout_ref, g_ref, b_ref, o_ref, *, n_split, n_sub):
    fc = D_FF // n_split
    sub = x_ref.shape[0] // n_sub
    for i in range(n_sub):
        rows = slice(i * sub, (i + 1) * sub)
        x = x_ref[rows, :]
        xb = x.astype(BF16)
        acc = None
        for f in range(n_split):
            gate = jnp.dot(xb, win_ref[:, f * fc:(f + 1) * fc], preferred_element_type=F32)
            up = jnp.dot(xb, win_ref[:, D_FF + f * fc:D_FF + (f + 1) * fc], preferred_element_type=F32)
            h = (gate * jax.nn.sigmoid(gate) * up).astype(BF16)
            y = jnp.dot(h, wout_ref[f * fc:(f + 1) * fc, :], preferred_element_type=F32)
            acc = y if acc is None else acc + y
        o_ref[rows, :] = _layer_norm(ALPHA * x + 0.5 * acc, g_ref[...], b_ref[...], LN_EPS)


def _ffn_block(x, w_in, w_out, g, b, *, tm=1024, n_split=11, n_sub=2):
    n = x.shape[0]
    return pl.pallas_call(
        functools.partial(_ffn_kernel, n_split=n_split, n_sub=n_sub),
        grid=(n // tm,),
        in_specs=[pl.BlockSpec((tm, D_MODEL), lambda i: (i, 0)),
                  _const_spec((D_MODEL, 2 * D_FF)),
                  _const_spec((D_FF, D_MODEL)),
                  _const_spec((1, D_MODEL)),
                  _const_spec((1, D_MODEL))],
        out_specs=pl.BlockSpec((tm, D_MODEL), lambda i: (i, 0)),
        out_shape=jax.ShapeDtypeStruct((n, D_MODEL), F32),
        compiler_params=_cp(1),
        name="ffn_block",
    )(x, w_in, w_out, g, b)


def _inproj_kernel(x_ref, wa_ref, oa_ref):
    oa_ref[...] = jnp.dot(x_ref[...].astype(BF16), wa_ref[...], preferred_element_type=F32)


def _in_proj(x, wa, *, tm=1024):
    n = x.shape[0]
    return pl.pallas_call(
        _inproj_kernel,
        grid=(n // tm,),
        in_specs=[pl.BlockSpec((tm, D_MODEL), lambda i: (i, 0)),
                  _const_spec((D_MODEL, PA_COLS))],
        out_specs=pl.BlockSpec((tm, PA_COLS), lambda i: (i, 0)),
        out_shape=jax.ShapeDtypeStruct((n, PA_COLS), F32),
        compiler_params=_cp(1),
        name="in_proj",
    )(x, wa)


def _tri_masks(reverse):
    row = lax.broadcasted_iota(jnp.int32, (CHUNK, CHUNK), 0)
    col = lax.broadcasted_iota(jnp.int32, (CHUNK, CHUNK), 1)
    incl = (col >= row) if reverse else (col <= row)
    strict = (col > row) if reverse else (col < row)
    return row, col, incl, strict


def _cum_consts(tb):
    idx = jnp.arange(tb)
    same = (idx[:, None] // CHUNK) == (idx[None, :] // CHUNK)
    fwd = same & (idx[None, :] <= idx[:, None])
    bwd = same & (idx[None, :] >= idx[:, None])
    return fwd.astype(BF16), bwd.astype(BF16)


def _mm01(m01, x):
    hi = x.astype(BF16)
    r1 = x - hi.astype(F32)
    mid = r1.astype(BF16)
    lo = (r1 - mid.astype(F32)).astype(BF16)
    dot = lambda p: jnp.dot(m01, p, preferred_element_type=F32)
    return dot(hi) + dot(mid) + dot(lo)


def _gla_kernel(qf_ref, kf_ref, vf_ref, gf_ref, qb_ref, kb_ref, vb_ref, gb_ref,
                upf_ref, bf_ref, upb_ref, bb_ref, cf_ref, cb_ref, of_ref, ob_ref, s_ref, *, tb):
    @pl.when(pl.program_id(1) == 0)
    def _():
        s_ref[...] = jnp.zeros_like(s_ref)

    n_chunk = tb // CHUNK
    n_pair = GLA_HEADS // 2
    pv = 2 * GLA_DV
    streams = ((qf_ref, kf_ref, vf_ref, gf_ref, upf_ref, bf_ref, cf_ref),
               (qb_ref, kb_ref, vb_ref, gb_ref, upb_ref, bb_ref, cb_ref))
    qt_t, kt_t, qs_t, ks_t, v_t, dec_t = [], [], [], [], [], []
    for d, (q_ref, k_ref, v_ref, g_ref, up_ref, gbias_ref, tri_ref) in enumerate(streams):
        z = _mm(g_ref[...], up_ref[...]) + gbias_ref[...]
        log_a = -_softplus(-z) / GLA_TAU
        b = _mm01(tri_ref[...], log_a)
        q = q_ref[...] * (GLA_DK ** -0.5)
        k = k_ref[...]
        v = v_ref[...].astype(BF16)
        for c in range(n_chunk):
            rows = slice(c * CHUNK, (c + 1) * CHUNK)
            end = c * CHUNK + (CHUNK - 1 if d == 0 else 0)
            half = 0.5 * b[end:end + 1, :]
            e_half = jnp.exp(half)
            qt = q[rows, :] * jnp.exp(b[rows, :] - half)
            kt = k[rows, :] * jnp.exp(half - b[rows, :])
            for p in range(n_pair):
                lanes = slice(p * PAIR, (p + 1) * PAIR)
                qt_t.append(qt[:, lanes])
                kt_t.append(kt[:, lanes])
                qs_t.append(qt[:, lanes] * e_half[:, lanes])
                ks_t.append(kt[:, lanes] * e_half[:, lanes])
                dec_t.append(e_half[:, lanes] * e_half[:, lanes])
                v_t.append(v[rows, p * pv:(p + 1) * pv])
    qt_s, kt_s, qs_s, ks_s, v_s = (jnp.stack(x) for x in (qt_t, kt_t, qs_t, ks_t, v_t))

    incl_f = _pair_masks(False)[0]
    incl_b = _pair_masks(True)[0]
    scores = _dir_where(incl_f, incl_b, _bmm(qt_s, _block_diag(kt_s), 2, 2))
    v_rows = lax.broadcasted_iota(jnp.int32, (PAIR, pv), 0) // CHUNK
    v_lanes = lax.broadcasted_iota(jnp.int32, (PAIR, pv), 1) // GLA_DV
    v_bd = jnp.where((v_rows == v_lanes)[None], jnp.concatenate([v_s, v_s], axis=1), jnp.zeros((), BF16))
    intra = _bmm(scores, v_bd, 2, 1)
    same_head = (lax.broadcasted_iota(jnp.int32, (pv, PAIR), 0) // GLA_DV
                 == lax.broadcasted_iota(jnp.int32, (pv, PAIR), 1) // GLA_DK)
    kv = jnp.where(same_head[None], _bmm(v_s, ks_s, 1, 1), 0.0)

    idx = lambda d, c, p: (d * n_chunk + c) * n_pair + p
    s_prev = [None] * (2 * n_chunk * n_pair)
    for d in range(2):
        for p in range(n_pair):
            state = s_ref[d * n_pair + p]
            for i in range(n_chunk):
                c = i if d == 0 else n_chunk - 1 - i
                s_prev[idx(d, c, p)] = state
                state = state * dec_t[idx(d, c, p)] + kv[idx(d, c, p)]
            s_ref[d * n_pair + p] = state
    out = intra + _bmm(qs_s, jnp.stack(s_prev), 2, 2)
    for d, o_ref in enumerate((of_ref, ob_ref)):
        for c in range(n_chunk):
            for p in range(n_pair):
                o_ref[c * CHUNK:(c + 1) * CHUNK, p * pv:(p + 1) * pv] = out[idx(d, c, p)].astype(BF16)


def _gla(pa, gd, up_f, bias_f, up_b, bias_b, *, tb=256):
    bsz, t_len, _ = pa.shape
    nt = t_len // tb
    cat_f, cat_b = _cum_consts(tb)

    def stream(tmap):
        return [pl.BlockSpec((None, tb, GLA_QK), lambda b, t: (b, tmap(t), 0)),
                pl.BlockSpec((None, tb, GLA_QK), lambda b, t: (b, tmap(t), 1)),
                pl.BlockSpec((None, tb, GLA_W), lambda b, t: (b, tmap(t), 1)),
                pl.BlockSpec((None, tb, 128), lambda b, t: (b, tmap(t), 0))]

    fwd = lambda t: t
    bwd = lambda t: nt - 1 - t
    out_shape = jax.ShapeDtypeStruct((bsz, t_len, GLA_W), BF16)
    return pl.pallas_call(
        functools.partial(_gla_kernel, tb=tb),
        grid=(bsz, nt),
        in_specs=stream(fwd) + stream(bwd)
                 + [_const_spec((128, GLA_QK)), _const_spec((1, GLA_QK)),
                    _const_spec((128, GLA_QK)), _const_spec((1, GLA_QK)),
                    _const_spec((tb, tb)), _const_spec((tb, tb))],
        out_specs=[pl.BlockSpec((None, tb, GLA_W), lambda b, t: (b, fwd(t), 0)),
                   pl.BlockSpec((None, tb, GLA_W), lambda b, t: (b, bwd(t), 0))],
        out_shape=[out_shape, out_shape],
        scratch_shapes=[pltpu.VMEM((GLA_HEADS, 2 * GLA_DV, 2 * GLA_DK), F32)],
        compiler_params=_cp(2),
        name="gla",
    )(pa, pa, pa, gd, pa, pa, pa, gd, up_f, bias_f, up_b, bias_b, cat_f, cat_b)


def _seg_sum(x, seg):
    hi = x.astype(BF16)
    lo = (x - hi.astype(F32)).astype(BF16)
    w = seg.shape[0]
    halves = [jnp.dot(hi[:, c:c + w], seg, preferred_element_type=F32)
              + jnp.dot(lo[:, c:c + w], seg, preferred_element_type=F32) for c in range(0, x.shape[1], w)]
    return jnp.concatenate(halves, axis=1)


def _prep_kernel(x_ref, xp_ref, xn_ref, wpb_ref, mup_ref, mun_ref, w0f_ref, upf_ref, w0b_ref, upb_ref,
                 a0_ref, aup_ref, gup_ref, kk_ref, ka_ref, rk_ref, seg_ref, cf_ref, cb_ref,
                 v_o, gate_o, bonus_o, gd_o, knf_o, rf_o, bf_o, kf_o, etf_o, knb_o, rb_o, bb_o, kb_o, etb_o,
                 *, tb, nt, sub):
    t = pl.program_id(1)
    halo_prev = jnp.where(t > 0, xp_ref[...], 0.0)
    halo_next = jnp.where(t < nt - 1, xn_ref[...], 0.0)
    x_ext = jnp.concatenate([halo_prev, x_ref[...], halo_next], axis=0).astype(BF16)
    starts = range(0, tb, sub)
    p_exts = [jnp.dot(x_ext[r0:r0 + sub + 16], wpb_ref[...], preferred_element_type=F32) for r0 in starts]
    seg = seg_ref[...]
    dirs = ((w0f_ref, upf_ref, cf_ref, knf_o, rf_o, bf_o, kf_o, etf_o),
            (w0b_ref, upb_ref, cb_ref, knb_o, rb_o, bb_o, kb_o, etb_o))
    rws = []
    for p_ext in p_exts:
        p = p_ext[8:8 + sub, :RW_P_COLS]
        rws.append(p + mup_ref[...] * (p_ext[7:7 + sub, :RW_P_COLS] - p)
                   + mun_ref[...] * (p_ext[9:9 + sub, :RW_P_COLS] - p))
    rs = [rw[:, 0:512] for rw in rws]
    krs = [rw[:, 512:1024] for rw in rws]
    vrs = [rw[:, 1024:1536] for rw in rws]
    wds = [jnp.tanh(rw[:, 1536:1664]) for rw in rws]
    a_pre = [_mm(rw[:, 1664:1792], aup_ref[...]) for rw in rws]
    gates = [_mm(jax.nn.sigmoid(rw[:, 1792:1920]), gup_ref[...]) for rw in rws]
    lw_pre = [[_mm(wd, dirs[d][1][...]) for wd in wds] for d in range(2)]
    kks = [kr * kk_ref[...] for kr in krs]
    kk_ss = [_seg_sum(kk * kk, seg) for kk in kks]
    a_s = [jax.nn.sigmoid(a0_ref[...] + ap) for ap in a_pre]
    k2s = [kr * (1.0 + (a - 1.0) * ka_ref[...]) for kr, a in zip(krs, a_s)]
    rk_ss = [_seg_sum(r * k2 * rk_ref[...], seg) for r, k2 in zip(rs, k2s)]
    lws = [[-DECAY_SCALE * jax.nn.sigmoid(dirs[d][0][...] + x) for x in lw_pre[d]] for d in range(2)]
    css = [[_mm01(dirs[d][2][...], lw) for lw in lws[d]] for d in range(2)]
    for i, r0 in enumerate(starts):
        rows = slice(r0, r0 + sub)
        gate_o[rows, :] = gates[i].astype(BF16)
        gd_o[rows, :] = p_exts[i][8:8 + sub, RW_P_COLS:]
        bonus_o[rows, :] = (rk_ss[i] * vrs[i]).astype(BF16)
        v_o[rows, :] = vrs[i].astype(BF16)
        kn = kks[i] * jnp.minimum(lax.rsqrt(kk_ss[i]), 1e12)
        b = a_s[i] * kn
        for d, (_, _, _, kn_o, r_o, b_o, k_o, et_o) in enumerate(dirs):
            cs = css[d][i]
            e_neg = jnp.exp(-cs)
            kn_o[rows, :] = (kn * jnp.exp(cs - lws[d][i])).astype(BF16)
            r_o[rows, :] = (rs[i] * jnp.exp(cs)).astype(BF16)
            b_o[rows, :] = (b * e_neg).astype(BF16)
            k_o[rows, :] = (k2s[i] * e_neg).astype(BF16)
            for c in range(sub // CHUNK):
                end = c * CHUNK + (CHUNK - 1 if d == 0 else 0)
                et_o[r0 // CHUNK + c] = jnp.exp(cs[end:end + 1, :])


def _rwkv_prep(x, prm, *, tb=1024, sub=256):
    bsz, t_len, _ = x.shape
    nt = t_len // tb
    hb = tb // 8
    n8 = t_len // 8
    n_chunk = tb // CHUNK
    cat_f, cat_b = _cum_consts(sub)
    consts = [prm["w_pb"], prm["mu_prev"], prm["mu_next"], prm["w0_f"], prm["up_f"], prm["w0_b"], prm["up_b"],
              prm["a0"], prm["a_up"], prm["g_up"], prm["k_k"], prm["k_a"], prm["r_k"], prm["seg"],
              cat_f, cat_b]
    row_spec = pl.BlockSpec((None, tb, RW_W), lambda b, t: (b, t, 0))
    rows = lambda dt: jax.ShapeDtypeStruct((bsz, t_len, RW_W), dt)
    et_spec = pl.BlockSpec((None, n_chunk, 1, RW_W), lambda b, t: (b, t, 0, 0))
    et = jax.ShapeDtypeStruct((bsz, t_len // CHUNK, 1, RW_W), F32)
    per_dir_specs = [row_spec] * 4 + [et_spec]
    per_dir_shapes = [rows(BF16)] * 4 + [et]
    return pl.pallas_call(
        functools.partial(_prep_kernel, tb=tb, nt=nt, sub=sub),
        grid=(bsz, nt),
        in_specs=[pl.BlockSpec((None, tb, D_MODEL), lambda b, t: (b, t, 0)),
                  pl.BlockSpec((None, 8, D_MODEL), lambda b, t: (b, jnp.maximum(t * hb - 1, 0), 0)),
                  pl.BlockSpec((None, 8, D_MODEL), lambda b, t: (b, jnp.minimum((t + 1) * hb, n8 - 1), 0))]
                 + [_const_spec(c.shape) for c in consts],
        out_specs=[row_spec] * 3 + [pl.BlockSpec((None, tb, 128), lambda b, t: (b, t, 0))] + per_dir_specs * 2,
        out_shape=[rows(BF16)] * 3 + [jax.ShapeDtypeStruct((bsz, t_len, 128), F32)] + per_dir_shapes * 2,
        compiler_params=_cp(2),
        name="rwkv_prep",
    )(x, x, x, *consts)


def _pair_masks(reverse):
    t = lax.broadcasted_iota(jnp.int32, (CHUNK, PAIR), 0)
    s = lax.broadcasted_iota(jnp.int32, (CHUNK, PAIR), 1) % RW_N
    incl = (s >= t) if reverse else (s <= t)
    strict = (s > t) if reverse else (s < t)
    levels = []
    m = 1
    while m < CHUNK:
        if reverse:
            levels.append(((t // m) % 2 == 0) & (s // m == t // m + 1))
        else:
            levels.append(((t // m) % 2 == 1) & (s // m == t // m - 1))
        m *= 2
    return incl, strict, levels


def _dir_where(mask_f, mask_b, x):
    h = x.shape[0] // 2
    return jnp.concatenate([jnp.where(mask_f[None], x[:h], 0.0), jnp.where(mask_b[None], x[h:], 0.0)], axis=0)


def _block_diag(x):
    rows = lax.broadcasted_iota(jnp.int32, (2 * CHUNK, PAIR), 0) // CHUNK
    lanes = lax.broadcasted_iota(jnp.int32, (2 * CHUNK, PAIR), 1) // RW_N
    x = x.astype(BF16)
    return jnp.where((rows == lanes)[None], jnp.concatenate([x, x], axis=1), jnp.zeros((), BF16))


def _chunk_operators(kn, r, b, k, v, e_tot):
    incl_f, strict_f, levels_f = _pair_masks(False)
    incl_b, strict_b, levels_b = _pair_masks(True)
    mask2_f = jnp.concatenate([strict_f, incl_f], axis=0)
    mask2_b = jnp.concatenate([strict_b, incl_b], axis=0)
    t = lax.broadcasted_iota(jnp.int32, (CHUNK, PAIR), 0)
    s = lax.broadcasted_iota(jnp.int32, (CHUNK, PAIR), 1) % RW_N
    eye = (t == s).astype(F32)
    same_head = (lax.broadcasted_iota(jnp.int32, (PAIR, PAIR), 0) // RW_N
                 == lax.broadcasted_iota(jnp.int32, (PAIR, PAIR), 1) // RW_N)[None]
    xr = jnp.concatenate([kn, r], axis=1)
    bk = jnp.concatenate([_block_diag(b), _block_diag(k)], axis=1)
    abk = _dir_where(jnp.concatenate([mask2_f, mask2_f], axis=1), jnp.concatenate([mask2_b, mask2_b], axis=1),
                     _bmm(xr, bk, 2, 2))
    ab = abk[:, :, :PAIR]
    ak = abk[:, :, PAIR:]
    av = _bmm(ak, _block_diag(v), 2, 1)
    a_b = ab[:, :CHUNK]
    a_rb = ab[:, CHUNK:]
    inv = eye[None] - _dir_where(levels_f[0], levels_b[0], a_b)
    for mk_f, mk_b in zip(levels_f[1:], levels_b[1:]):
        x = _bmm(_dir_where(mk_f, mk_b, a_b), _block_diag(inv), 2, 1)
        inv = inv - _bmm(inv, _block_diag(x), 2, 1)
    gw = _bmm(inv, jnp.concatenate([_block_diag(kn), _block_diag(av[:, :CHUNK])], axis=2), 2, 1)
    g = gw[:, :, :PAIR]
    w = gw[:, :, PAIR:]
    corr = _bmm(a_rb, jnp.concatenate([_block_diag(g), _block_diag(w)], axis=2), 2, 1)
    r_op = r.astype(F32) - corr[:, :, :PAIR]
    y0 = av[:, CHUNK:] - corr[:, :, PAIR:]
    p_op = jnp.where(same_head, _bmm(g, b, 1, 1), 0.0) * (-e_tot)
    vw = jnp.concatenate([v, w.astype(BF16)], axis=1)
    kb = jnp.concatenate([k, -b], axis=1)
    q_op = jnp.where(same_head, _bmm(vw, kb, 1, 1), 0.0) * e_tot
    return r_op, y0, p_op, q_op


def _scan_kernel(knf_ref, rf_ref, bf_ref, kf_ref, vf_ref, etf_ref,
                 knb_ref, rb_ref, bb_ref, kb_ref, vb_ref, etb_ref, yf_ref, yb_ref, s_ref, *, tb):
    @pl.when(pl.program_id(1) == 0)
    def _():
        s_ref[...] = jnp.zeros_like(s_ref)

    n_chunk = tb // CHUNK
    streams = ((knf_ref, rf_ref, bf_ref, kf_ref, vf_ref), (knb_ref, rb_ref, bb_ref, kb_ref, vb_ref))

    def stacked(ref_f, ref_b, rows):
        tiles = []
        for ref in (ref_f, ref_b):
            for c in range(n_chunk):
                x = ref[c * rows:(c + 1) * rows, :] if rows == CHUNK else ref[c]
                tiles += [x[:, p * PAIR:(p + 1) * PAIR] for p in range(N_PAIR)]
        return jnp.stack(tiles)

    ops = [stacked(streams[0][i], streams[1][i], CHUNK) for i in range(5)]
    e_tot = stacked(etf_ref, etb_ref, 1)
    r_op, y0, p_op, q_op = _chunk_operators(*ops, e_tot)

    state = s_ref[...]
    half = n_chunk * N_PAIR
    for i in range(n_chunk):
        cf, cb = i, n_chunk - 1 - i
        sel = lambda x: jnp.concatenate([x[cf * N_PAIR:(cf + 1) * N_PAIR],
                                         x[half + cb * N_PAIR:half + (cb + 1) * N_PAIR]], axis=0)
        y = _bmm(sel(r_op), state, 2, 2) + sel(y0)
        state = state * sel(e_tot) + _bmm(state, sel(p_op), 2, 1) + sel(q_op)
        for p in range(N_PAIR):
            yf_ref[cf * CHUNK:(cf + 1) * CHUNK, p * PAIR:(p + 1) * PAIR] = y[p].astype(BF16)
            yb_ref[cb * CHUNK:(cb + 1) * CHUNK, p * PAIR:(p + 1) * PAIR] = y[N_PAIR + p].astype(BF16)
    s_ref[...] = state


def _rwkv_scan(v, fwd_ops, bwd_ops, *, tb=512):
    bsz, t_len, _ = v.shape
    nt = t_len // tb
    n_chunk = tb // CHUNK

    def stream(tmap):
        rows = pl.BlockSpec((None, tb, RW_W), lambda bi, ti: (bi, tmap(ti), 0))
        et = pl.BlockSpec((None, n_chunk, 1, RW_W), lambda bi, ti: (bi, tmap(ti), 0, 0))
        return rows, [rows] * 5 + [et]

    rows_f, specs_f = stream(lambda t: t)
    rows_b, specs_b = stream(lambda t: nt - 1 - t)
    out_shape = jax.ShapeDtypeStruct(v.shape, BF16)
    kn_f, r_f, b_f, k_f, et_f = fwd_ops
    kn_b, r_b, b_b, k_b, et_b = bwd_ops
    return pl.pallas_call(
        functools.partial(_scan_kernel, tb=tb),
        grid=(bsz, nt),
        in_specs=specs_f + specs_b,
        out_specs=[rows_f, rows_b],
        out_shape=[out_shape, out_shape],
        scratch_shapes=[pltpu.VMEM((2 * N_PAIR, PAIR, PAIR), F32)],
        compiler_params=_cp(2),
        name="rwkv_scan",
    )(kn_f, r_f, b_f, k_f, v, et_f, kn_b, r_b, b_b, k_b, v, et_b)


def _mixout_kernel(of_ref, ob_ref, g_ref, gn_ref, yf_ref, yb_ref, gate_ref, bonus_ref, lg_ref, lb_ref,
                   seg_ref, x_ref, w_ref, ng_ref, nb_ref, o_ref, *, n_sub):
    seg = seg_ref[...]
    sub = x_ref.shape[0] // n_sub
    tiles = [slice(i * sub, (i + 1) * sub) for i in range(n_sub)]
    f32 = lambda ref, rows: ref[rows, :].astype(F32)
    ys = [f32(yf_ref, rows) + f32(yb_ref, rows) for rows in tiles]
    ycs = [y - _seg_sum(y, seg) * (1.0 / RW_N) for y in ys]
    variances = [_seg_sum(yc * yc, seg) * (1.0 / RW_N) for yc in ycs]
    outs = []
    for rows, yc, var in zip(tiles, ycs, variances):
        o = f32(of_ref, rows) + f32(ob_ref, rows)
        parts = []
        for h in range(GLA_HEADS):
            oh = o[:, h * GLA_DV:(h + 1) * GLA_DV]
            parts.append(oh * lax.rsqrt(jnp.mean(oh * oh, axis=-1, keepdims=True) + LN_EPS))
        g = g_ref[rows, :]
        o = jnp.concatenate(parts, axis=-1) * gn_ref[...] * (g * jax.nn.sigmoid(g))
        y = yc * lax.rsqrt(var + RW_GN_EPS) * lg_ref[...] + lb_ref[...]
        y = (y + f32(bonus_ref, rows)) * f32(gate_ref, rows)
        mixed = jnp.concatenate([o, y], axis=-1).astype(BF16)
        outs.append(jnp.dot(mixed, w_ref[...], preferred_element_type=F32))
    for rows, tm_out in zip(tiles, outs):
        o_ref[rows, :] = _layer_norm(ALPHA * x_ref[rows, :] + tm_out, ng_ref[...], nb_ref[...], LN_EPS)


def _mix_out(o_f, o_b, pa, gla_norm_g, y_f, y_b, gate, bonus, lnx_g, lnx_b, seg, x, w_out, ln_g, ln_b,
             *, tm=1024):
    bsz, t_len, _ = x.shape
    row = lambda width: pl.BlockSpec((None, tm, width), lambda b, t: (b, t, 0))
    return pl.pallas_call(
        functools.partial(_mixout_kernel, n_sub=4),
        grid=(bsz, t_len // tm),
        in_specs=[row(GLA_W), row(GLA_W),
                  pl.BlockSpec((None, tm, GLA_W), lambda b, t: (b, t, 2)),
                  _const_spec((1, GLA_W)),
                  row(RW_W), row(RW_W), row(RW_W), row(RW_W),
                  _const_spec((1, RW_W)), _const_spec((1, RW_W)), _const_spec((RW_W // 2, RW_W // 2)),
                  row(D_MODEL),
                  _const_spec((D_MODEL, D_MODEL)),
                  _const_spec((1, D_MODEL)), _const_spec((1, D_MODEL))],
        out_specs=row(D_MODEL),
        out_shape=jax.ShapeDtypeStruct(x.shape, F32),
        compiler_params=_cp(2),
        name="mix_out",
    )(o_f, o_b, pa, gla_norm_g, y_f, y_b, gate, bonus, lnx_g, lnx_b, seg, x, w_out, ln_g, ln_b)


def _memkv_kernel(m_ref, g_ref, b_ref, w_ref, k_ref, v_ref):
    m = _layer_norm(m_ref[...], g_ref[...], b_ref[...], LN_EPS)
    kv = jnp.dot(m.astype(BF16), w_ref[...], preferred_element_type=F32)
    k_ref[...] = kv[:, :D_MODEL].astype(BF16)
    v_ref[...] = kv[:, D_MODEL:].astype(BF16)


def _mem_kv(mem, g, b, w_kv):
    bsz, m_tok, _ = mem.shape
    spec = pl.BlockSpec((None, m_tok, D_MODEL), lambda i: (i, 0, 0))
    shape = jax.ShapeDtypeStruct(mem.shape, BF16)
    return pl.pallas_call(
        _memkv_kernel,
        grid=(bsz,),
        in_specs=[spec, _const_spec((1, D_MODEL)), _const_spec((1, D_MODEL)),
                  _const_spec((D_MODEL, 2 * D_MODEL))],
        out_specs=[spec, spec],
        out_shape=[shape, shape],
        compiler_params=_cp(1),
        name="mem_kv",
    )(mem, g, b, w_kv)


def _ca_kernel(x_ref, k_ref, v_ref, wq_ref, wo_ref, g_ref, b_ref, o_ref, *, n_sub):
    k = k_ref[...]
    v = v_ref[...]
    sub = x_ref.shape[0] // n_sub
    heads = [slice(h * MEM_HD, (h + 1) * MEM_HD) for h in range(MEM_HEADS)]
    xs = [x_ref[i * sub:(i + 1) * sub, :] for i in range(n_sub)]
    qs = [jnp.dot(x.astype(BF16), wq_ref[...], preferred_element_type=F32).astype(BF16) for x in xs]
    scores = [[_mm(q[:, hs], k[:, hs], _NT) * (MEM_HD ** -0.5) for hs in heads] for q in qs]
    outs = []
    for i in range(n_sub):
        parts = []
        for s, hs in zip(scores[i], heads):
            e = jnp.exp(s - jnp.max(s, axis=-1, keepdims=True))
            p = e * (1.0 / jnp.sum(e, axis=-1, keepdims=True))
            parts.append(_mm(p, v[:, hs], _NN).astype(BF16))
        outs.append(jnp.dot(jnp.concatenate(parts, axis=-1), wo_ref[...], preferred_element_type=F32))
    for i in range(n_sub):
        o_ref[i * sub:(i + 1) * sub, :] = _layer_norm(ALPHA * xs[i] + outs[i], g_ref[...], b_ref[...], LN_EPS)


def _cross_attn(x, k, v, w_q, w_o, g, b, *, tm=1024):
    bsz, t_len, _ = x.shape
    m_tok = k.shape[1]
    row = pl.BlockSpec((None, tm, D_MODEL), lambda bi, t: (bi, t, 0))
    mem = pl.BlockSpec((None, m_tok, D_MODEL), lambda bi, t: (bi, 0, 0))
    return pl.pallas_call(
        functools.partial(_ca_kernel, n_sub=4),
        grid=(bsz, t_len // tm),
        in_specs=[row, mem, mem, _const_spec((D_MODEL, D_MODEL)), _const_spec((D_MODEL, D_MODEL)),
                  _const_spec((1, D_MODEL)), _const_spec((1, D_MODEL))],
        out_specs=row,
        out_shape=jax.ShapeDtypeStruct(x.shape, F32),
        compiler_params=_cp(2),
        name="cross_attn",
    )(x, k, v, w_q, w_o, g, b)


def _pad_rows(w, start, total):
    return jnp.zeros((total, w.shape[1]), w.dtype).at[start:start + w.shape[0]].set(w)


def _layer(x, mem, p):
    bsz, t_len, _ = x.shape
    x2d = x.reshape(bsz * t_len, D_MODEL)
    x1 = _ffn_block(x2d, p["ffn1_w_in"], p["ffn1_w_out"], p["ln_ffn1_g"], p["ln_ffn1_b"])
    x1_seq = x1.reshape(bsz, t_len, D_MODEL)
    pa = _in_proj(x1, p["w_pa"]).reshape(bsz, t_len, PA_COLS)
    v, gate, bonus, gd, *ops = _rwkv_prep(x1_seq, p["rw"])
    o_f, o_b = _gla(pa, gd, p["gla_up_f"], p["gla_b_f"], p["gla_up_b"], p["gla_b_b"])
    y_f, y_b = _rwkv_scan(v, ops[:5], ops[5:])
    x2 = _mix_out(o_f, o_b, pa, p["gla_norm_g"], y_f, y_b, gate, bonus, p["lnx_g"], p["lnx_b"], p["rw"]["seg"],
                  x1_seq, p["w_out"], p["ln_mix_g"], p["ln_mix_b"])
    mk, mv = _mem_kv(mem, p["mem_ln_g"], p["mem_ln_b"], p["ca_w_kv"])
    x3 = _cross_attn(x2, mk, mv, p["ca_w_q"], p["ca_w_o"], p["ln_ca_g"], p["ln_ca_b"])
    x4 = _ffn_block(x3.reshape(bsz * t_len, D_MODEL), p["ffn2_w_in"], p["ffn2_w_out"],
                    p["ln_ffn2_g"], p["ln_ffn2_b"])
    return x4.reshape(bsz, t_len, D_MODEL)


def kernel(x_prompt, x_sample, mem_prompt, mem_sample, ffn1_w_in, ffn1_w_out, ln_ffn1_g, ln_ffn1_b, mix_w_in, gla_gate_up_fwd, gla_gate_b_fwd, gla_gate_up_bwd, gla_gate_b_bwd, gla_norm_g, rwkv_mu_prev, rwkv_mu_next, rwkv_w0_fwd, rwkv_w_up_fwd, rwkv_w0_bwd, rwkv_w_up_bwd, rwkv_a0, rwkv_a_up, rwkv_g_up, rwkv_k_k, rwkv_k_a, rwkv_r_k, rwkv_lnx_g, rwkv_lnx_b, mix_w_out, ln_mix_g, ln_mix_b, mem_ln_g, mem_ln_b, ca_w_q, ca_w_kv, ca_w_o, ln_ca_g, ln_ca_b, ffn2_w_in, ffn2_w_out, ln_ffn2_g, ln_ffn2_b):
    y_prompt, y_sample = x_prompt, x_sample
    for l in range(DEPTH):
        row = lambda a: a[l].reshape(1, -1)
        w_in = mix_w_in[l]
        zeros = lambda n: jnp.zeros((D_MODEL, n), F32)
        rw_off = GLA_COLS
        lr_off = rw_off + 3 * RW_W
        gd_off = lr_off + 2 * RW_DECAY_RANK + RW_AAA_RANK
        w_pa = w_in[:, :PA_COLS]
        w_pb = jnp.concatenate([w_in[:, rw_off:gd_off], zeros(64), w_in[:, gd_off:],
                                w_in[:, PA_COLS:GLA_COLS], zeros(PB_COLS - RW_P_COLS - 2 * GLA_RANK)], axis=1)
        perm_mu = lambda mu: jnp.concatenate(
            [mu[:gd_off - rw_off], jnp.zeros((64,), F32), mu[gd_off - rw_off:]]).reshape(1, RW_P_COLS)
        rw = {
            "w_pb": w_pb.astype(BF16),
            "mu_prev": perm_mu(rwkv_mu_prev[l]), "mu_next": perm_mu(rwkv_mu_next[l]),
            "w0_f": row(rwkv_w0_fwd), "up_f": _pad_rows(rwkv_w_up_fwd[l], 0, 128).astype(BF16),
            "w0_b": row(rwkv_w0_bwd), "up_b": _pad_rows(rwkv_w_up_bwd[l], RW_DECAY_RANK, 128).astype(BF16),
            "a0": row(rwkv_a0), "a_up": _pad_rows(rwkv_a_up[l], 0, 128).astype(BF16),
            "g_up": rwkv_g_up[l].astype(BF16),
            "k_k": row(rwkv_k_k), "k_a": row(rwkv_k_a), "r_k": rwkv_r_k[l].reshape(1, RW_W),
            "seg": jnp.kron(jnp.eye(RW_HEADS // 2, dtype=F32), jnp.ones((RW_N, RW_N), F32)).astype(BF16),
        }
        p = {
            "ffn1_w_in": ffn1_w_in[l].astype(BF16), "ffn1_w_out": ffn1_w_out[l].astype(BF16),
            "ln_ffn1_g": row(ln_ffn1_g), "ln_ffn1_b": row(ln_ffn1_b),
            "w_pa": w_pa.astype(BF16),
            "gla_up_f": _pad_rows(gla_gate_up_fwd[l], 0, 128), "gla_b_f": row(gla_gate_b_fwd),
            "gla_up_b": _pad_rows(gla_gate_up_bwd[l], GLA_RANK, 128), "gla_b_b": row(gla_gate_b_bwd),
            "gla_norm_g": row(gla_norm_g),
            "rw": rw,
            "lnx_g": row(rwkv_lnx_g), "lnx_b": row(rwkv_lnx_b),
            "w_out": mix_w_out[l].astype(BF16),
            "ln_mix_g": row(ln_mix_g), "ln_mix_b": row(ln_mix_b),
            "mem_ln_g": row(mem_ln_g), "mem_ln_b": row(mem_ln_b),
            "ca_w_q": ca_w_q[l].astype(BF16), "ca_w_kv": ca_w_kv[l].astype(BF16), "ca_w_o": ca_w_o[l].astype(BF16),
            "ln_ca_g": row(ln_ca_g), "ln_ca_b": row(ln_ca_b),
            "ffn2_w_in": ffn2_w_in[l].astype(BF16), "ffn2_w_out": ffn2_w_out[l].astype(BF16),
            "ln_ffn2_g": row(ln_ffn2_g), "ln_ffn2_b": row(ln_ffn2_b),
        }
        y_prompt = _layer(y_prompt, mem_prompt, p)
        y_sample = _layer(y_sample, mem_sample, p)
    return (y_prompt, y_sample)
```
